```python
import jax, jax.numpy as jnp
from jax import lax
import numpy as np

D_MODEL = 2048
BATCH = 2
SEQ = 8192
DEPTH = 1

N_MEM = 256
EPS = 1e-6
CONV_WIDTH = D_MODEL
CONV_K = 31
GLA_HEADS = 4
GLA_DK = D_MODEL // 2
GLA_DV = D_MODEL
GLA_DKH = GLA_DK // GLA_HEADS
GLA_DVH = GLA_DV // GLA_HEADS
GLA_RANK = 16
GLA_TAU = 16.0
CHUNK = 64
MEM_HEADS = 4
MEM_HD = 128
MEM_WIDTH = MEM_HEADS * MEM_HD
N_BRANCH = 3

SPLIT_SIZES = [
    CONV_WIDTH,
    CONV_WIDTH,
    CONV_WIDTH,
    GLA_DK,
    GLA_DK,
    GLA_DV,
    GLA_DV,
    GLA_RANK,
    MEM_WIDTH,
    MEM_WIDTH,
    N_BRANCH * D_MODEL,
]
SPLITS = [int(s) for s in np.cumsum(SPLIT_SIZES)[:-1]]
D_IN = int(sum(SPLIT_SIZES))

kernel_name = "hybrid_conv_gla_memxattn_gated_merge"


def rmsnorm(x, g):
    xf = x.astype(jnp.float32)
    y = xf * lax.rsqrt(jnp.mean(xf * xf, axis=-1, keepdims=True) + EPS)
    return (y * g.astype(jnp.float32)).astype(x.dtype)


def layernorm(x, g, b):
    xf = x.astype(jnp.float32)
    mu = jnp.mean(xf, axis=-1, keepdims=True)
    var = jnp.mean(jnp.square(xf - mu), axis=-1, keepdims=True)
    y = (xf - mu) * lax.rsqrt(var + EPS)
    return (y * g.astype(jnp.float32) + b.astype(jnp.float32)).astype(x.dtype)


def causal_depthwise_conv(u, w, b):
    C = u.shape[-1]
    y = lax.conv_general_dilated(
        u, w[:, None, :].astype(u.dtype), window_strides=(1,),
        padding=[(CONV_K - 1, 0)],
        dimension_numbers=("NWC", "WIO", "NWC"),
        feature_group_count=C)
    return y + b.astype(u.dtype)


def gla_chunked(q, k, v, log_a):
    B, S, H, dk = q.shape
    dv = v.shape[-1]
    N = S // CHUNK

    def to_chunks(t):
        return t.astype(jnp.float32).reshape(B, N, CHUNK, H, t.shape[-1]).transpose(1, 0, 3, 2, 4)

    qc, kc, vc = to_chunks(q), to_chunks(k), to_chunks(v)
    bc = jnp.cumsum(to_chunks(log_a), axis=-2)
    mask = jnp.tril(jnp.ones((CHUNK, CHUNK), dtype=bool))[:, :, None]

    def step(state, inp):
        qi, ki, vi, bi = inp
        o_inter = jnp.einsum("bhtk,bhkv->bhtv", qi * jnp.exp(bi), state)
        diff = bi[:, :, :, None, :] - bi[:, :, None, :, :]
        decay = jnp.exp(jnp.where(mask, diff, -jnp.inf))
        scores = jnp.einsum("bhtk,bhsk,bhtsk->bhts", qi, ki, decay)
        o_intra = jnp.einsum("bhts,bhsv->bhtv", scores, vi)
        b_last = bi[:, :, -1, :]
        k_dec = ki * jnp.exp(b_last[:, :, None, :] - bi)
        state = jnp.exp(b_last)[..., None] * state + jnp.einsum("bhck,bhcv->bhkv", k_dec, vi)
        return state, o_inter + o_intra

    s0 = jnp.zeros((B, H, dk, dv), jnp.float32)
    _, o = lax.scan(step, s0, (qc, kc, vc, bc))
    return o.transpose(1, 0, 3, 2, 4).reshape(B, S, H, dv)


def setup_inputs(seed: int = 0) -> dict:
    key = jax.random.key(seed)
    ks = jax.random.split(key, 24)
    f32 = jnp.float32
    L = DEPTH

    def nrm(k, shape, scale):
        return jax.random.normal(k, shape, f32) * scale

    return {
        "x": jax.random.normal(ks[0], (BATCH, SEQ, D_MODEL), f32),
        "mem": jax.random.normal(ks[1], (BATCH, N_MEM, D_MODEL), f32),
        "ln_in_g": 1.0 + nrm(ks[2], (L, D_MODEL), 0.02),
        "mem_ln_g": 1.0 + nrm(ks[3], (L, D_MODEL), 0.02),
        "w_in": nrm(ks[4], (L, D_MODEL, D_IN), D_MODEL ** -0.5),
        "b_gate": nrm(ks[5], (L, N_BRANCH * D_MODEL), 0.02),
        "dw_w": nrm(ks[6], (L, CONV_K, CONV_WIDTH), CONV_K ** -0.5),
        "dw_b": nrm(ks[7], (L, CONV_WIDTH), 0.02),
        "conv_ln_g": 1.0 + nrm(ks[8], (L, CONV_WIDTH), 0.02),
        "conv_ln_b": nrm(ks[9], (L, CONV_WIDTH), 0.02),
        "w_conv_out": nrm(ks[10], (L, CONV_WIDTH, D_MODEL), CONV_WIDTH ** -0.5),
        "b_conv_out": nrm(ks[11], (L, D_MODEL), 0.02),
        "w_alpha2": nrm(ks[12], (L, GLA_RANK, GLA_DK), GLA_RANK ** -0.5),
        "b_alpha": nrm(ks[13], (L, GLA_DK), 0.02),
        "gla_norm_g": 1.0 + nrm(ks[14], (L, GLA_DV), 0.02),
        "w_gla_out": nrm(ks[15], (L, GLA_DV, D_MODEL), GLA_DV ** -0.5),
        "w_mem_kv": nrm(ks[16], (L, D_MODEL, 2 * MEM_WIDTH), D_MODEL ** -0.5),
        "w_mem_out": nrm(ks[17], (L, MEM_WIDTH, D_MODEL), MEM_WIDTH ** -0.5),
        "w_out": nrm(ks[18], (L, D_MODEL, D_MODEL), D_MODEL ** -0.5),
        "final_g": 1.0 + nrm(ks[19], (D_MODEL,), 0.02),
    }


def reference(x, mem, ln_in_g, mem_ln_g, w_in, b_gate, dw_w, dw_b, conv_ln_g, conv_ln_b,
              w_conv_out, b_conv_out, w_alpha2, b_alpha, gla_norm_g, w_gla_out,
              w_mem_kv, w_mem_out, w_out, final_g):
    B, S, D = x.shape
    M = mem.shape[1]
    for l in range(DEPTH):
        h = rmsnorm(x, ln_in_g[l])
        proj = h @ w_in[l]
        (conv_a, conv_b, conv_z, q, k, v, gla_z, alpha_lr, mq, mz, gate_pre) = jnp.split(proj, SPLITS, axis=-1)

        u = conv_a * jax.nn.sigmoid(conv_b)
        u = causal_depthwise_conv(u, dw_w[l], dw_b[l])
        u = jax.nn.silu(layernorm(u, conv_ln_g[l], conv_ln_b[l]))
        u = u * jax.nn.silu(conv_z)
        y_conv = u @ w_conv_out[l] + b_conv_out[l]

        log_a = jax.nn.log_sigmoid((alpha_lr @ w_alpha2[l] + b_alpha[l]).astype(jnp.float32)) / GLA_TAU
        qh = q.reshape(B, S, GLA_HEADS, GLA_DKH) * (GLA_DKH ** -0.5)
        kh = k.reshape(B, S, GLA_HEADS, GLA_DKH)
        vh = v.reshape(B, S, GLA_HEADS, GLA_DVH)
        o = gla_chunked(qh, kh, vh, log_a.reshape(B, S, GLA_HEADS, GLA_DKH)).astype(x.dtype)
        o = rmsnorm(o, gla_norm_g[l].reshape(GLA_HEADS, GLA_DVH)).reshape(B, S, GLA_DV)
        y_gla = (o * jax.nn.silu(gla_z)) @ w_gla_out[l]

        m = rmsnorm(mem, mem_ln_g[l])
        mk, mv = jnp.split(m @ w_mem_kv[l], 2, axis=-1)
        mk = mk.reshape(B, M, MEM_HEADS, MEM_HD)
        mv = mv.reshape(B, M, MEM_HEADS, MEM_HD)
        mqh = mq.reshape(B, S, MEM_HEADS, MEM_HD)
        sc = jnp.einsum("bshd,bmhd->bhsm", mqh.astype(jnp.float32), mk.astype(jnp.float32)) * (MEM_HD ** -0.5)
        p = jax.nn.softmax(sc, axis=-1).astype(x.dtype)
        om = jnp.einsum("bhsm,bmhd->bshd", p, mv).reshape(B, S, MEM_WIDTH)
        y_mem = (om * jax.nn.silu(mz)) @ w_mem_out[l]

        g = jax.nn.sigmoid(gate_pre + b_gate[l]).reshape(B, S, N_BRANCH, D)
        y = g[:, :, 0] * y_conv + g[:, :, 1] * y_gla + g[:, :, 2] * y_mem
        x = x + y @ w_out[l]
    return rmsnorm(x, final_g)
```

```python
import functools

import numpy as np
import jax
import jax.numpy as jnp
from jax import lax
from jax.experimental import pallas as pl
from jax.experimental.pallas import tpu as pltpu

F32 = jnp.float32
BF16 = jnp.bfloat16

D_MODEL = 2048
EPS = 1e-6
CONV_K = 31
GLA_HEADS = 4
GLA_DK = 1024
GLA_DV = 2048
GLA_DKH = GLA_DK // GLA_HEADS
GLA_DVH = GLA_DV // GLA_HEADS
GLA_RANK = 16
GLA_TAU = 16.0
MEM_HEADS = 4
MEM_HD = 128
MEM_WIDTH = MEM_HEADS * MEM_HD
N_BRANCH = 3

LANES = 128
VMEM_LIMIT = 56 * 1024 * 1024

GLA_CHUNK = 256
GLA_LEVELS = 8
RANK_PAD = LANES
CONV_HALO = 32
CONV_TB = 8


def _params(sem):
    return pltpu.CompilerParams(dimension_semantics=sem, vmem_limit_bytes=VMEM_LIMIT)


def _resident(shape):
    nd = len(shape)
    return pl.BlockSpec(shape, lambda *_: (0,) * nd, pipeline_mode=pl.Buffered(1))


def _sigmoid(x):
    return 1.0 / (1.0 + jnp.exp(-x))


def _silu(x):
    return x * _sigmoid(x)


def _dot(a, b):
    return jnp.dot(a, b, preferred_element_type=F32)


def _dot_nt(a, b):
    return lax.dot_general(a, b, (((1,), (1,)), ((), ())), preferred_element_type=F32)


def _dot_tn(a, b):
    return lax.dot_general(a, b, (((0,), (0,)), ((), ())), preferred_element_type=F32)


def _rmsnorm_kernel(x_ref, g_ref, o_ref):
    x = x_ref[...]
    ms = jnp.mean(x * x, axis=-1, keepdims=True)
    o_ref[...] = (x * lax.rsqrt(ms + EPS) * g_ref[...]).astype(o_ref.dtype)


def _rmsnorm_bf16(x, g, tm):
    t, d = x.shape
    return pl.pallas_call(
        _rmsnorm_kernel,
        grid=(t // tm,),
        in_specs=[pl.BlockSpec((tm, d), lambda i: (i, 0)), _resident((1, d))],
        out_specs=pl.BlockSpec((tm, d), lambda i: (i, 0)),
        out_shape=jax.ShapeDtypeStruct((t, d), BF16),
        compiler_params=_params(("parallel",)),
        name="rmsnorm_in",
    )(x, g.reshape(1, d))


def _proj_conv_kernel(h_ref, wa_ref, wb_ref, wz_ref, u_ref, sz_ref):
    h = h_ref[...]
    a = _dot(h, wa_ref[...])
    b = _dot(h, wb_ref[...])
    u_ref[...] = (a * _sigmoid(b)).astype(BF16)
    z = _dot(h, wz_ref[...])
    sz_ref[...] = _silu(z).astype(BF16)


def _proj_qk_kernel(h_ref, wq_ref, wk_ref, q_ref, k_ref):
    h = h_ref[...]
    q_ref[...] = (_dot(h, wq_ref[...]) * (GLA_DKH ** -0.5)).astype(BF16)
    k_ref[...] = _dot(h, wk_ref[...]).astype(BF16)


def _proj_vz_kernel(h_ref, wv_ref, wz_ref, v_ref, sz_ref):
    h = h_ref[...]
    v_ref[...] = _dot(h, wv_ref[...]).astype(BF16)
    sz_ref[...] = _silu(_dot(h, wz_ref[...])).astype(BF16)


def _proj_mem_kernel(h_ref, wq_ref, wz_ref, wal_ref, q_ref, sz_ref, al_ref):
    h = h_ref[...]
    q_ref[...] = _dot(h, wq_ref[...]).astype(BF16)
    sz_ref[...] = _silu(_dot(h, wz_ref[...])).astype(BF16)
    al_ref[...] = _dot(h, wal_ref[...])


def _proj_gate_kernel(h_ref, w_ref, b_ref, g_ref):
    g_ref[...] = _sigmoid(_dot(h_ref[...], w_ref[...]) + b_ref[...]).astype(BF16)


def _proj_call(kernel, name, h, weights, out_dtypes, tm, tn, bias=None):
    t, d = h.shape
    n = weights[0].shape[1]
    assert all(w.shape == (d, n) for w in weights)
    in_specs = [pl.BlockSpec((tm, d), lambda i, j: (i, 0))]
    in_specs += [pl.BlockSpec((d, tn), lambda i, j: (0, j)) for _ in weights]
    args = [h, *weights]
    if bias is not None:
        in_specs.append(pl.BlockSpec((1, tn), lambda i, j: (0, j)))
        args.append(bias.reshape(1, n))
    return pl.pallas_call(
        kernel,
        grid=(t // tm, n // tn),
        in_specs=in_specs,
        out_specs=[pl.BlockSpec((tm, tn), lambda i, j: (i, j)) for _ in out_dtypes],
        out_shape=[jax.ShapeDtypeStruct((t, n), dt) for dt in out_dtypes],
        compiler_params=_params(("parallel", "arbitrary")),
        name=name,
    )(*args)


def _dwconv_kernel(u_ref, w_ref, b_ref, o_ref, ext_ref, *, ts):
    s = pl.program_id(1)

    @pl.when(s == 0)
    def _():
        ext_ref[0:CONV_HALO] = jnp.zeros((CONV_HALO,) + ext_ref.shape[1:], F32)

    @pl.when(s > 0)
    def _():
        ext_ref[0:CONV_HALO] = ext_ref[ts:ts + CONV_HALO]

    ext_ref[CONV_HALO:CONV_HALO + ts] = u_ref[...].astype(F32)
    lead = CONV_HALO - (CONV_K - 1)

    def body(i, carry):
        t0 = pl.multiple_of(i * CONV_TB, CONV_TB)
        acc = jnp.broadcast_to(b_ref[...], (CONV_TB,) + ext_ref.shape[1:])
        for j in range(CONV_K):
            acc = acc + w_ref[j] * ext_ref[pl.ds(t0 + lead + j, CONV_TB)]
        o_ref[pl.ds(t0, CONV_TB)] = acc.astype(o_ref.dtype)
        return carry

    lax.fori_loop(0, ts // CONV_TB, body, 0)


def _dwconv(u, w, b, batch, seq, ts):
    t, c = u.shape
    sub = c // LANES
    u3 = u.reshape(t, sub, LANES)
    n_s = seq // ts
    out = pl.pallas_call(
        functools.partial(_dwconv_kernel, ts=ts),
        grid=(batch, n_s),
        in_specs=[pl.BlockSpec((ts, sub, LANES), lambda bi, si: (bi * n_s + si, 0, 0)),
                  _resident((CONV_K, sub, LANES)),
                  _resident((1, sub, LANES))],
        out_specs=pl.BlockSpec((ts, sub, LANES), lambda bi, si: (bi * n_s + si, 0, 0)),
        out_shape=jax.ShapeDtypeStruct((t, sub, LANES), BF16),
        scratch_shapes=[pltpu.VMEM((ts + CONV_HALO, sub, LANES), F32)],
        compiler_params=_params(("arbitrary", "arbitrary")),
        name="dwconv",
    )(u3, w.reshape(CONV_K, sub, LANES), b.reshape(1, sub, LANES))
    return out.reshape(t, c)


def _conv_post_kernel(c_ref, sz_ref, g_ref, lg_ref, lb_ref, w_ref, bo_ref, o_ref):
    c = c_ref[...].astype(F32)
    mu = jnp.mean(c, axis=-1, keepdims=True)
    d = c - mu
    var = jnp.mean(d * d, axis=-1, keepdims=True)
    yh = d * lax.rsqrt(var + EPS) * lg_ref[...] + lb_ref[...]
    u = _silu(yh) * sz_ref[...].astype(F32)
    yc = _dot(u.astype(BF16), w_ref[...]) + bo_ref[...]
    o_ref[...] = (yc * g_ref[...].astype(F32)).astype(o_ref.dtype)


def _conv_post(c, sz, gates, ln_g, ln_b, w, b_out, tm):
    t, d = c.shape
    row = lambda i: (i, 0)
    return pl.pallas_call(
        _conv_post_kernel,
        grid=(t // tm,),
        in_specs=[pl.BlockSpec((tm, d), row), pl.BlockSpec((tm, d), row),
                  pl.BlockSpec((tm, d), lambda i: (i, 0)),
                  _resident((1, d)), _resident((1, d)), _resident((d, d)), _resident((1, d))],
        out_specs=pl.BlockSpec((tm, d), row),
        out_shape=jax.ShapeDtypeStruct((t, d), BF16),
        compiler_params=_params(("parallel",)),
        name="conv_post",
    )(c, sz, gates, ln_g.reshape(1, d), ln_b.reshape(1, d), w, b_out.reshape(1, d))


def _gla_constants():
    c = GLA_CHUNK
    t = np.arange(c)[:, None]
    u = np.arange(c)[None, :]
    mats = np.zeros((GLA_LEVELS + 1, 2 * c, c), np.float32)
    mats[0, :c] = u <= t
    mats[0, c:] = u > t
    for lvl in range(GLA_LEVELS):
        m = 1 << lvl
        pos = t % (2 * m)
        r = t - pos + m - 1
        mats[1 + lvl, :c] = (pos >= m) & (u > r) & (u <= t)
        mats[1 + lvl, c:] = (pos < m) & (u > t) & (u <= r)
    tt = np.arange(c)[:, None]
    ss = np.arange(c)[None, :]
    x = tt ^ ss
    lv = np.where(x > 0, np.floor(np.log2(np.maximum(x, 1))).astype(np.int32), GLA_LEVELS)
    lv = np.where(ss > tt, -1, lv).astype(np.int32)
    return mats, lv


def _gla_kernel(q_ref, k_ref, v_ref, al_ref, w2_ref, ba_ref, gz_ref, ng_ref, mats_ref, lv_ref,
                o_ref, st_ref):
    c = GLA_CHUNK

    @pl.when(pl.program_id(2) == 0)
    def _():
        st_ref[...] = jnp.zeros_like(st_ref)

    al = al_ref[...]
    al_hi = al.astype(BF16)
    al_lo = (al - al_hi.astype(F32)).astype(BF16)
    w2 = w2_ref[...]
    w2_hi = w2.astype(BF16)
    w2_lo = (w2 - w2_hi.astype(F32)).astype(BF16)
    x = _dot(al_hi, w2_hi) + _dot(al_lo, w2_hi) + _dot(al_hi, w2_lo) + ba_ref[...]
    log_a = (jnp.minimum(x, 0.0) - jnp.log1p(jnp.exp(-jnp.abs(x)))) * (1.0 / GLA_TAU)
    la = log_a.astype(BF16)

    q = q_ref[...]
    k = k_ref[...]
    v = v_ref[...]
    qf = q.astype(F32)
    kf = k.astype(F32)

    e0 = _dot(mats_ref[0], la)
    b = e0[:c]
    q_in = (qf * jnp.exp(b)).astype(BF16)
    k_up = (kf * jnp.exp(e0[c:])).astype(BF16)

    st = st_ref[...]
    o = _dot_nt(q_in, st.astype(BF16))

    lv = lv_ref[...]
    p = jnp.where(lv == GLA_LEVELS, _dot_nt(q, k), 0.0)
    for lvl in range(GLA_LEVELS):
        e = _dot(mats_ref[1 + lvl], la)
        ql = (qf * jnp.exp(e[:c])).astype(BF16)
        kl = (kf * jnp.exp(e[c:])).astype(BF16)
        p = jnp.where(lv == lvl, _dot_nt(ql, kl), p)
    o = o + _dot(p.astype(BF16), v)

    st_ref[...] = st * jnp.exp(b[c - 1:c, :]) + _dot_tn(v, k_up)

    ms = jnp.mean(o * o, axis=-1, keepdims=True)
    y = o * lax.rsqrt(ms + EPS) * ng_ref[...]
    o_ref[...] = (y * gz_ref[...].astype(F32)).astype(o_ref.dtype)


def _gla(q, k, v, alpha, w_alpha2, b_alpha, gz, norm_g, batch, seq):
    t = q.shape[0]
    c = GLA_CHUNK
    n_c = seq // c
    mats, lv = _gla_constants()
    w2 = jnp.zeros((RANK_PAD, GLA_DK), F32).at[:GLA_RANK].set(w_alpha2)
    tok = lambda bi, hi, ni: (bi * n_c + ni, hi)
    head = lambda bi, hi, ni: (0, hi)
    return pl.pallas_call(
        _gla_kernel,
        grid=(batch, GLA_HEADS, n_c),
        in_specs=[pl.BlockSpec((c, GLA_DKH), tok), pl.BlockSpec((c, GLA_DKH), tok),
                  pl.BlockSpec((c, GLA_DVH), tok),
                  pl.BlockSpec((c, RANK_PAD), lambda bi, hi, ni: (bi * n_c + ni, 0)),
                  pl.BlockSpec((RANK_PAD, GLA_DKH), head), pl.BlockSpec((1, GLA_DKH), head),
                  pl.BlockSpec((c, GLA_DVH), tok), pl.BlockSpec((1, GLA_DVH), head),
                  _resident(mats.shape), _resident(lv.shape)],
        out_specs=pl.BlockSpec((c, GLA_DVH), tok),
        out_shape=jax.ShapeDtypeStruct((t, GLA_DV), BF16),
        scratch_shapes=[pltpu.VMEM((GLA_DVH, GLA_DKH), F32)],
        compiler_params=_params(("arbitrary", "arbitrary", "arbitrary")),
        name="gla",
    )(q, k, v, alpha, w2, b_alpha.reshape(1, GLA_DK), gz, norm_g.reshape(1, GLA_DV),
      jnp.asarray(mats, BF16), jnp.asarray(lv))


def _gated_proj_kernel(x_ref, w_ref, g_ref, o_ref):
    o_ref[...] = (_dot(x_ref[...], w_ref[...]) * g_ref[...].astype(F32)).astype(o_ref.dtype)


def _gated_proj(x, w, gates, branch, tm, name):
    t, kdim = x.shape
    d = w.shape[1]
    return pl.pallas_call(
        _gated_proj_kernel,
        grid=(t // tm,),
        in_specs=[pl.BlockSpec((tm, kdim), lambda i: (i, 0)), _resident((kdim, d)),
                  pl.BlockSpec((tm, d), lambda i: (i, branch))],
        out_specs=pl.BlockSpec((tm, d), lambda i: (i, 0)),
        out_shape=jax.ShapeDtypeStruct((t, d), BF16),
        compiler_params=_params(("parallel",)),
        name=name,
    )(x, w, gates)


def _mem_kv_kernel(m_ref, g_ref, w_ref, o_ref):
    m = m_ref[...]
    ms = jnp.mean(m * m, axis=-1, keepdims=True)
    mn = (m * lax.rsqrt(ms + EPS) * g_ref[...]).astype(BF16)
    o_ref[...] = _dot(mn, w_ref[...]).astype(o_ref.dtype)


def _mem_kv(mem2, g, w, batch, n_mem):
    d = mem2.shape[1]
    n = w.shape[1]
    return pl.pallas_call(
        _mem_kv_kernel,
        grid=(batch,),
        in_specs=[pl.BlockSpec((n_mem, d), lambda bi: (bi, 0)), _resident((1, d)), _resident((d, n))],
        out_specs=pl.BlockSpec((n_mem, n), lambda bi: (bi, 0)),
        out_shape=jax.ShapeDtypeStruct((batch * n_mem, n), BF16),
        compiler_params=_params(("parallel",)),
        name="mem_kv",
    )(mem2, g.reshape(1, d), w)


def _mem_attn_kernel(q_ref, sz_ref, kv_ref, w_ref, g_ref, o_ref):
    q = q_ref[...]
    kv = kv_ref[...]
    outs = []
    for hd in range(MEM_HEADS):
        lo, hi = hd * MEM_HD, (hd + 1) * MEM_HD
        sc = _dot_nt(q[:, lo:hi], kv[:, lo:hi]) * (MEM_HD ** -0.5)
        sc = sc - jnp.max(sc, axis=-1, keepdims=True)
        e = jnp.exp(sc)
        p = e / jnp.sum(e, axis=-1, keepdims=True)
        outs.append(_dot(p.astype(BF16), kv[:, MEM_WIDTH + lo:MEM_WIDTH + hi]))
    om = jnp.concatenate(outs, axis=-1) * sz_ref[...].astype(F32)
    y = _dot(om.astype(BF16), w_ref[...])
    o_ref[...] = (y * g_ref[...].astype(F32)).astype(o_ref.dtype)


def _mem_attn(mq, smz, mkv, w_out, gates, batch, seq, n_mem, tm):
    t = mq.shape[0]
    d = w_out.shape[1]
    n_t = seq // tm
    tok = lambda bi, ti: (bi * n_t + ti, 0)
    return pl.pallas_call(
        _mem_attn_kernel,
        grid=(batch, n_t),
        in_specs=[pl.BlockSpec((tm, MEM_WIDTH), tok), pl.BlockSpec((tm, MEM_WIDTH), tok),
                  pl.BlockSpec((n_mem, 2 * MEM_WIDTH), lambda bi, ti: (bi, 0)),
                  _resident((MEM_WIDTH, d)),
                  pl.BlockSpec((tm, d), lambda bi, ti: (bi * n_t + ti, 2))],
        out_specs=pl.BlockSpec((tm, d), tok),
        out_shape=jax.ShapeDtypeStruct((t, d), BF16),
        compiler_params=_params(("parallel", "parallel")),
        name="mem_attn",
    )(mq, smz, mkv, w_out, gates)


def _merge_kernel(y0_ref, y1_ref, y2_ref, x_ref, w_ref, g_ref, o_ref):
    y = y0_ref[...].astype(F32) + y1_ref[...].astype(F32) + y2_ref[...].astype(F32)
    r = x_ref[...] + _dot(y.astype(BF16), w_ref[...])
    ms = jnp.mean(r * r, axis=-1, keepdims=True)
    o_ref[...] = r * lax.rsqrt(ms + EPS) * g_ref[...]


def _merge(y0, y1, y2, x, w, g, tm):
    t, d = x.shape
    row = lambda i: (i, 0)
    return pl.pallas_call(
        _merge_kernel,
        grid=(t // tm,),
        in_specs=[pl.BlockSpec((tm, d), row)] * 4 + [_resident((d, d)), _resident((1, d))],
        out_specs=pl.BlockSpec((tm, d), row),
        out_shape=jax.ShapeDtypeStruct((t, d), F32),
        compiler_params=_params(("parallel",)),
        name="merge_out",
    )(y0, y1, y2, x, w, g.reshape(1, d))


def kernel(x, mem, ln_in_g, mem_ln_g, w_in, b_gate, dw_w, dw_b, conv_ln_g, conv_ln_b,
           w_conv_out, b_conv_out, w_alpha2, b_alpha, gla_norm_g, w_gla_out,
           w_mem_kv, w_mem_out, w_out, final_g):
    batch, seq, d = x.shape
    n_mem = mem.shape[1]
    depth = w_in.shape[0]
    t = batch * seq
    xs = x.reshape(t, d)
    mem2 = mem.reshape(batch * n_mem, d)

    for l in range(depth):
        w = w_in[l]
        offs = np.cumsum([0, d, d, d, GLA_DK, GLA_DK, GLA_DV, GLA_DV, GLA_RANK,
                          MEM_WIDTH, MEM_WIDTH, N_BRANCH * d])
        seg = [w[:, int(offs[i]):int(offs[i + 1])] for i in range(len(offs) - 1)]
        (w_ca, w_cb, w_cz, w_q, w_k, w_v, w_gz, w_al, w_mq, w_mz, w_g) = seg
        bf = lambda a: a.astype(BF16)
        w_al_pad = jnp.zeros((d, RANK_PAD), BF16).at[:, :GLA_RANK].set(bf(w_al))

        h = _rmsnorm_bf16(xs, ln_in_g[l], tm=512)

        u, conv_sz = _proj_call(_proj_conv_kernel, "proj_conv", h, [bf(w_ca), bf(w_cb), bf(w_cz)],
                                [BF16, BF16], tm=1024, tn=512)
        q, k = _proj_call(_proj_qk_kernel, "proj_qk", h, [bf(w_q), bf(w_k)], [BF16, BF16],
                          tm=1024, tn=512)
        v, gla_sz = _proj_call(_proj_vz_kernel, "proj_vz", h, [bf(w_v), bf(w_gz)], [BF16, BF16],
                               tm=1024, tn=512)
        mq, mem_sz, alpha = pl.pallas_call(
            _proj_mem_kernel,
            grid=(t // 1024,),
            in_specs=[pl.BlockSpec((1024, d), lambda i: (i, 0)), _resident((d, MEM_WIDTH)),
                      _resident((d, MEM_WIDTH)), _resident((d, RANK_PAD))],
            out_specs=[pl.BlockSpec((1024, MEM_WIDTH), lambda i: (i, 0)),
                       pl.BlockSpec((1024, MEM_WIDTH), lambda i: (i, 0)),
                       pl.BlockSpec((1024, RANK_PAD), lambda i: (i, 0))],
            out_shape=[jax.ShapeDtypeStruct((t, MEM_WIDTH), BF16),
                       jax.ShapeDtypeStruct((t, MEM_WIDTH), BF16),
                       jax.ShapeDtypeStruct((t, RANK_PAD), F32)],
            compiler_params=_params(("parallel",)),
            name="proj_mem",
        )(h, bf(w_mq), bf(w_mz), w_al_pad)
        (gates,) = _proj_call(_proj_gate_kernel, "proj_gate", h, [bf(w_g)], [BF16],
                              tm=1024, tn=1024, bias=b_gate[l])

        conv = _dwconv(u, dw_w[l], dw_b[l], batch, seq, ts=512)
        y_conv = _conv_post(conv, conv_sz, gates, conv_ln_g[l], conv_ln_b[l],
                            bf(w_conv_out[l]), b_conv_out[l], tm=512)

        og = _gla(q, k, v, alpha, w_alpha2[l], b_alpha[l], gla_sz, gla_norm_g[l], batch, seq)
        y_gla = _gated_proj(og, bf(w_gla_out[l]), gates, 1, tm=512, name="gla_out")

        mkv = _mem_kv(mem2, mem_ln_g[l], bf(w_mem_kv[l]), batch, n_mem)
        y_mem = _mem_attn(mq, mem_sz, mkv, bf(w_mem_out[l]), gates, batch, seq, n_mem, tm=512)

        xs = _merge(y_conv, y_gla, y_mem, xs, bf(w_out[l]), final_g, tm=512)
        assert depth == 1
    return xs.reshape(batch, seq, d)
```

```python
import functools

import numpy as np
import jax
import jax.numpy as jnp
from jax import lax
from jax.experimental import pallas as pl
from jax.experimental.pallas import tpu as pltpu

F32 = jnp.float32
BF16 = jnp.bfloat16

D_MODEL = 2048
EPS = 1e-6
CONV_K = 31
GLA_HEADS = 4
GLA_DK = 1024
GLA_DV = 2048
GLA_DKH = GLA_DK // GLA_HEADS
GLA_DVH = GLA_DV // GLA_HEADS
GLA_RANK = 16
GLA_TAU = 16.0
MEM_HEADS = 4
MEM_HD = 128
MEM_WIDTH = MEM_HEADS * MEM_HD
N_BRANCH = 3

LANES = 128
VMEM_LIMIT = 56 * 1024 * 1024

GLA_CHUNK = 256
GLA_LEVELS = 8
RANK_PAD = LANES
CONV_HALO = 32
CONV_TB = 8
LOG2_E = 1.4426950408889634


def _params(sem):
    return pltpu.CompilerParams(dimension_semantics=sem, vmem_limit_bytes=VMEM_LIMIT)


def _resident(shape):
    nd = len(shape)
    return pl.BlockSpec(shape, lambda *_: (0,) * nd, pipeline_mode=pl.Buffered(1))


def _sigmoid(x):
    return 1.0 / (1.0 + jnp.exp(-x))


def _silu(x):
    return x * _sigmoid(x)


def _dot(a, b):
    return jnp.dot(a, b, preferred_element_type=F32)


def _dot_nt(a, b):
    return lax.dot_general(a, b, (((1,), (1,)), ((), ())), preferred_element_type=F32)


def _dot_tn(a, b):
    return lax.dot_general(a, b, (((0,), (0,)), ((), ())), preferred_element_type=F32)


def _split_bf16(x):
    hi = x.astype(BF16)
    return hi, (x - hi.astype(F32)).astype(BF16)


def _rms_scale(x):
    return lax.rsqrt(jnp.mean(x * x, axis=-1, keepdims=True) + EPS)


def _rmsnorm_kernel(x_ref, g_ref, o_ref):
    x = x_ref[...]
    o_ref[...] = (x * _rms_scale(x) * g_ref[...]).astype(o_ref.dtype)


def _rmsnorm_bf16(x, g, tm):
    t, d = x.shape
    return pl.pallas_call(
        _rmsnorm_kernel,
        grid=(t // tm,),
        in_specs=[pl.BlockSpec((tm, d), lambda i: (i, 0)), _resident((1, d))],
        out_specs=pl.BlockSpec((tm, d), lambda i: (i, 0)),
        out_shape=jax.ShapeDtypeStruct((t, d), BF16),
        compiler_params=_params(("parallel",)),
        name="rmsnorm_in",
    )(x, g.reshape(1, d))


def _proj_conv_kernel(h_ref, wa_ref, wb_ref, wz_ref, u_ref, sz_ref):
    h = h_ref[...]
    a = _dot(h, wa_ref[...])
    b = _dot(h, wb_ref[...])
    u_ref[...] = (a * _sigmoid(b)).astype(BF16)
    z = _dot(h, wz_ref[...])
    sz_ref[...] = _silu(z).astype(BF16)


def _proj_qk_kernel(h_ref, wq_ref, wk_ref, q_ref, k_ref):
    h = h_ref[...]
    q_ref[...] = (_dot(h, wq_ref[...]) * (GLA_DKH ** -0.5)).astype(BF16)
    k_ref[...] = _dot(h, wk_ref[...]).astype(BF16)


def _proj_vz_kernel(h_ref, wv_ref, wz_ref, v_ref, sz_ref):
    h = h_ref[...]
    v_ref[...] = _dot(h, wv_ref[...]).astype(BF16)
    sz_ref[...] = _silu(_dot(h, wz_ref[...])).astype(BF16)


def _proj_mem_kernel(h_ref, wq_ref, wz_ref, wal_ref, w2_ref, ba_ref, q_ref, sz_ref, la_ref):
    h = h_ref[...]
    q_ref[...] = _dot(h, wq_ref[...]).astype(BF16)
    sz_ref[...] = _silu(_dot(h, wz_ref[...])).astype(BF16)
    al_hi, al_lo = _split_bf16(_dot(h, wal_ref[...]))
    w2_hi, w2_lo = _split_bf16(w2_ref[...])
    x = _dot(al_hi, w2_hi) + _dot(al_lo, w2_hi) + _dot(al_hi, w2_lo) + ba_ref[...]
    log_a = (jnp.minimum(x, 0.0) - jnp.log1p(jnp.exp(-jnp.abs(x)))) * (1.0 / GLA_TAU)
    la_ref[...] = (log_a * LOG2_E).astype(BF16)


def _proj_gate_kernel(h_ref, w_ref, b_ref, g_ref):
    g_ref[...] = _sigmoid(_dot(h_ref[...], w_ref[...]) + b_ref[...]).astype(BF16)


def _proj_call(kernel, name, h, weights, out_dtypes, tm, tn, bias=None):
    t, d = h.shape
    n = weights[0].shape[1]
    assert all(w.shape == (d, n) for w in weights)
    in_specs = [pl.BlockSpec((tm, d), lambda i, j: (i, 0))]
    in_specs += [pl.BlockSpec((d, tn), lambda i, j: (0, j)) for _ in weights]
    args = [h, *weights]
    if bias is not None:
        in_specs.append(pl.BlockSpec((1, tn), lambda i, j: (0, j)))
        args.append(bias.reshape(1, n))
    return pl.pallas_call(
        kernel,
        grid=(t // tm, n // tn),
        in_specs=in_specs,
        out_specs=[pl.BlockSpec((tm, tn), lambda i, j: (i, j)) for _ in out_dtypes],
        out_shape=[jax.ShapeDtypeStruct((t, n), dt) for dt in out_dtypes],
        compiler_params=_params(("parallel", "arbitrary")),
        name=name,
    )(*args)


def _proj_mem(h, w_mq, w_mz, w_al_pad, w2_pad, b_alpha, tm):
    t, d = h.shape
    row = lambda i: (i, 0)
    return pl.pallas_call(
        _proj_mem_kernel,
        grid=(t // tm,),
        in_specs=[pl.BlockSpec((tm, d), row), _resident((d, MEM_WIDTH)), _resident((d, MEM_WIDTH)),
                  _resident((d, RANK_PAD)), _resident((RANK_PAD, GLA_DK)), _resident((1, GLA_DK))],
        out_specs=[pl.BlockSpec((tm, MEM_WIDTH), row), pl.BlockSpec((tm, MEM_WIDTH), row),
                   pl.BlockSpec((tm, GLA_DK), row)],
        out_shape=[jax.ShapeDtypeStruct((t, MEM_WIDTH), BF16),
                   jax.ShapeDtypeStruct((t, MEM_WIDTH), BF16),
                   jax.ShapeDtypeStruct((t, GLA_DK), BF16)],
        compiler_params=_params(("parallel",)),
        name="proj_mem",
    )(h, w_mq, w_mz, w_al_pad, w2_pad, b_alpha.reshape(1, GLA_DK))


def _dwconv_kernel(u_ref, w_ref, b_ref, o_ref, ext_ref, *, ts):
    s = pl.program_id(1)

    @pl.when(s == 0)
    def _():
        ext_ref[0:CONV_HALO] = jnp.zeros((CONV_HALO,) + ext_ref.shape[1:], F32)

    @pl.when(s > 0)
    def _():
        ext_ref[0:CONV_HALO] = ext_ref[ts:ts + CONV_HALO]

    ext_ref[CONV_HALO:CONV_HALO + ts] = u_ref[...].astype(F32)
    lead = CONV_HALO - (CONV_K - 1)

    def body(i, carry):
        t0 = pl.multiple_of(i * CONV_TB, CONV_TB)
        acc = jnp.broadcast_to(b_ref[...], (CONV_TB,) + ext_ref.shape[1:])
        for j in range(CONV_K):
            acc = acc + w_ref[j] * ext_ref[pl.ds(t0 + lead + j, CONV_TB)]
        o_ref[pl.ds(t0, CONV_TB)] = acc.astype(o_ref.dtype)
        return carry

    lax.fori_loop(0, ts // CONV_TB, body, 0)


def _dwconv(u, w, b, batch, seq, ts):
    t, c = u.shape
    sub = c // LANES
    u3 = u.reshape(t, sub, LANES)
    n_s = seq // ts
    out = pl.pallas_call(
        functools.partial(_dwconv_kernel, ts=ts),
        grid=(batch, n_s),
        in_specs=[pl.BlockSpec((ts, sub, LANES), lambda bi, si: (bi * n_s + si, 0, 0)),
                  _resident((CONV_K, sub, LANES)),
                  _resident((1, sub, LANES))],
        out_specs=pl.BlockSpec((ts, sub, LANES), lambda bi, si: (bi * n_s + si, 0, 0)),
        out_shape=jax.ShapeDtypeStruct((t, sub, LANES), BF16),
        scratch_shapes=[pltpu.VMEM((ts + CONV_HALO, sub, LANES), F32)],
        compiler_params=_params(("arbitrary", "arbitrary")),
        name="dwconv",
    )(u3, w.reshape(CONV_K, sub, LANES), b.reshape(1, sub, LANES))
    return out.reshape(t, c)


def _gla_constants():
    c = GLA_CHUNK
    t = np.arange(c)[:, None]
    u = np.arange(c)[None, :]
    mats = np.zeros((GLA_LEVELS + 1, c, c), np.float32)
    mats[0] = u <= t
    for lvl in range(GLA_LEVELS):
        m = 1 << lvl
        pos = t % (2 * m)
        r = t - pos + m - 1
        mats[1 + lvl] = ((pos >= m) & (u > r) & (u <= t)) | ((pos < m) & (u > t) & (u <= r))
    x = t ^ u
    lv = np.where(x > 0, np.floor(np.log2(np.maximum(x, 1))).astype(np.int32), GLA_LEVELS)
    lv = np.where(u > t, -1, lv).astype(np.int32)
    return mats.reshape((GLA_LEVELS + 1) * c, c), lv


def _gla_kernel(q_ref, k_ref, v_ref, la_ref, gz_ref, ng_ref, mats_ref, lv_ref, o_ref, st_ref):
    c = GLA_CHUNK

    @pl.when(pl.program_id(1) == 0)
    def _():
        st_ref[...] = jnp.zeros_like(st_ref)

    lv = lv_ref[...]
    for hd in range(GLA_HEADS):
        ks = slice(hd * GLA_DKH, (hd + 1) * GLA_DKH)
        vs = slice(hd * GLA_DVH, (hd + 1) * GLA_DVH)
        q = q_ref[:, ks]
        k = k_ref[:, ks]
        v = v_ref[:, vs]
        qf = q.astype(F32)
        kf = k.astype(F32)

        e_all = _dot(mats_ref[...], la_ref[:, ks])
        b2 = e_all[:c]
        b2_last = b2[c - 1:c, :]
        q_in = (qf * jnp.exp2(b2)).astype(BF16)
        k_up = (kf * jnp.exp2(b2_last - b2)).astype(BF16)

        st = st_ref[hd]
        o = _dot_nt(q_in, st.astype(BF16))

        p = jnp.where(lv == GLA_LEVELS, _dot_nt(q, k), 0.0)
        for lvl in range(GLA_LEVELS):
            w = jnp.exp2(e_all[(1 + lvl) * c:(2 + lvl) * c])
            ql = (qf * w).astype(BF16)
            kl = (kf * w).astype(BF16)
            p = jnp.where(lv == lvl, _dot_nt(ql, kl), p)
        o = o + _dot(p.astype(BF16), v)

        st_ref[hd] = st * jnp.exp2(b2_last) + _dot_tn(v, k_up)

        y = o * _rms_scale(o) * ng_ref[:, vs]
        o_ref[:, vs] = (y * gz_ref[:, vs].astype(F32)).astype(o_ref.dtype)


def _gla(q, k, v, la2, gz, norm_g, batch, seq):
    t = q.shape[0]
    c = GLA_CHUNK
    n_c = seq // c
    mats, lv = _gla_constants()
    tok = lambda bi, ni: (bi * n_c + ni, 0)
    return pl.pallas_call(
        _gla_kernel,
        grid=(batch, n_c),
        in_specs=[pl.BlockSpec((c, GLA_DK), tok), pl.BlockSpec((c, GLA_DK), tok),
                  pl.BlockSpec((c, GLA_DV), tok), pl.BlockSpec((c, GLA_DK), tok),
                  pl.BlockSpec((c, GLA_DV), tok), _resident((1, GLA_DV)),
                  _resident(mats.shape), _resident(lv.shape)],
        out_specs=pl.BlockSpec((c, GLA_DV), tok),
        out_shape=jax.ShapeDtypeStruct((t, GLA_DV), BF16),
        scratch_shapes=[pltpu.VMEM((GLA_HEADS, GLA_DVH, GLA_DKH), F32)],
        compiler_params=_params(("arbitrary", "arbitrary")),
        name="gla",
    )(q, k, v, la2, gz, norm_g.reshape(1, GLA_DV), jnp.asarray(mats, BF16), jnp.asarray(lv))


def _mem_kv_kernel(m_ref, g_ref, w_ref, o_ref):
    m = m_ref[...]
    mn = (m * _rms_scale(m) * g_ref[...]).astype(BF16)
    o_ref[...] = _dot(mn, w_ref[...]).astype(o_ref.dtype)


def _mem_kv(mem2, g, w, batch, n_mem):
    d = mem2.shape[1]
    n = w.shape[1]
    return pl.pallas_call(
        _mem_kv_kernel,
        grid=(batch,),
        in_specs=[pl.BlockSpec((n_mem, d), lambda bi: (bi, 0)), _resident((1, d)), _resident((d, n))],
        out_specs=pl.BlockSpec((n_mem, n), lambda bi: (bi, 0)),
        out_shape=jax.ShapeDtypeStruct((batch * n_mem, n), BF16),
        compiler_params=_params(("parallel",)),
        name="mem_kv",
    )(mem2, g.reshape(1, d), w)


def _tail_kernel(c_ref, csz_ref, og_ref, mq_ref, msz_ref, kv_ref, g_ref, x_ref,
                 lg_ref, lb_ref, wc_ref, bc_ref, wg_ref, wm_ref, wo_ref, fg_ref, o_ref):
    d = D_MODEL
    c = c_ref[...].astype(F32)
    dev = c - jnp.mean(c, axis=-1, keepdims=True)
    var = jnp.mean(dev * dev, axis=-1, keepdims=True)
    yh = dev * lax.rsqrt(var + EPS) * lg_ref[...] + lb_ref[...]
    u = _silu(yh) * csz_ref[...].astype(F32)
    y = (_dot(u.astype(BF16), wc_ref[...]) + bc_ref[...]) * g_ref[:, 0:d].astype(F32)

    y = y + _dot(og_ref[...], wg_ref[...]) * g_ref[:, d:2 * d].astype(F32)

    q = mq_ref[...]
    kv = kv_ref[...]
    outs = []
    for hd in range(MEM_HEADS):
        lo, hi = hd * MEM_HD, (hd + 1) * MEM_HD
        sc = _dot_nt(q[:, lo:hi], kv[:, lo:hi]) * (MEM_HD ** -0.5)
        e = jnp.exp(sc - jnp.max(sc, axis=-1, keepdims=True))
        p = e / jnp.sum(e, axis=-1, keepdims=True)
        outs.append(_dot(p.astype(BF16), kv[:, MEM_WIDTH + lo:MEM_WIDTH + hi]))
    om = jnp.concatenate(outs, axis=-1) * msz_ref[...].astype(F32)
    y = y + _dot(om.astype(BF16), wm_ref[...]) * g_ref[:, 2 * d:3 * d].astype(F32)

    r = x_ref[...] + _dot(y.astype(BF16), wo_ref[...])
    o_ref[...] = r * _rms_scale(r) * fg_ref[...]


def _tail(conv, conv_sz, og, mq, mem_sz, mkv, gates, x, ln_g, ln_b, w_conv, b_conv, w_gla,
          w_mem, w_out, final_g, batch, seq, n_mem, tm):
    t, d = x.shape
    n_t = seq // tm
    tok = lambda bi, ti: (bi * n_t + ti, 0)
    wide = lambda w: pl.BlockSpec((tm, w), tok)
    return pl.pallas_call(
        _tail_kernel,
        grid=(batch, n_t),
        in_specs=[wide(d), wide(d), wide(GLA_DV), wide(MEM_WIDTH), wide(MEM_WIDTH),
                  pl.BlockSpec((n_mem, 2 * MEM_WIDTH), lambda bi, ti: (bi, 0)),
                  wide(N_BRANCH * d), wide(d),
                  _resident((1, d)), _resident((1, d)), _resident((d, d)), _resident((1, d)),
                  _resident((GLA_DV, d)), _resident((MEM_WIDTH, d)), _resident((d, d)),
                  _resident((1, d))],
        out_specs=wide(d),
        out_shape=jax.ShapeDtypeStruct((t, d), F32),
        compiler_params=_params(("parallel", "parallel")),
        name="tail",
    )(conv, conv_sz, og, mq, mem_sz, mkv, gates, x, ln_g.reshape(1, d), ln_b.reshape(1, d),
      w_conv, b_conv.reshape(1, d), w_gla, w_mem, w_out, final_g.reshape(1, d))


def kernel(x, mem, ln_in_g, mem_ln_g, w_in, b_gate, dw_w, dw_b, conv_ln_g, conv_ln_b,
           w_conv_out, b_conv_out, w_alpha2, b_alpha, gla_norm_g, w_gla_out,
           w_mem_kv, w_mem_out, w_out, final_g):
    batch, seq, d = x.shape
    n_mem = mem.shape[1]
    depth = w_in.shape[0]
    assert depth == 1 and d == D_MODEL
    t = batch * seq
    xs = x.reshape(t, d)
    mem2 = mem.reshape(batch * n_mem, d)
    bf = lambda a: a.astype(BF16)
    l = 0

    w = w_in[l]
    offs = np.cumsum([0, d, d, d, GLA_DK, GLA_DK, GLA_DV, GLA_DV, GLA_RANK,
                      MEM_WIDTH, MEM_WIDTH, N_BRANCH * d])
    seg = [w[:, int(offs[i]):int(offs[i + 1])] for i in range(len(offs) - 1)]
    (w_ca, w_cb, w_cz, w_q, w_k, w_v, w_gz, w_al, w_mq, w_mz, w_g) = seg
    w_al_pad = jnp.zeros((d, RANK_PAD), BF16).at[:, :GLA_RANK].set(bf(w_al))
    w2_pad = jnp.zeros((RANK_PAD, GLA_DK), F32).at[:GLA_RANK].set(w_alpha2[l])

    h = _rmsnorm_bf16(xs, ln_in_g[l], tm=512)

    u, conv_sz = _proj_call(_proj_conv_kernel, "proj_conv", h, [bf(w_ca), bf(w_cb), bf(w_cz)],
                            [BF16, BF16], tm=1024, tn=512)
    q, k = _proj_call(_proj_qk_kernel, "proj_qk", h, [bf(w_q), bf(w_k)], [BF16, BF16],
                      tm=1024, tn=1024)
    v, gla_sz = _proj_call(_proj_vz_kernel, "proj_vz", h, [bf(w_v), bf(w_gz)], [BF16, BF16],
                           tm=1024, tn=1024)
    mq, mem_sz, la2 = _proj_mem(h, bf(w_mq), bf(w_mz), w_al_pad, w2_pad, b_alpha[l], tm=1024)
    (gates,) = _proj_call(_proj_gate_kernel, "proj_gate", h, [bf(w_g)], [BF16],
                          tm=1024, tn=2048, bias=b_gate[l])

    conv = _dwconv(u, dw_w[l], dw_b[l], batch, seq, ts=512)
    og = _gla(q, k, v, la2, gla_sz, gla_norm_g[l], batch, seq)
    mkv = _mem_kv(mem2, mem_ln_g[l], bf(w_mem_kv[l]), batch, n_mem)

    out = _tail(conv, conv_sz, og, mq, mem_sz, mkv, gates, xs, conv_ln_g[l], conv_ln_b[l],
                bf(w_conv_out[l]), b_conv_out[l], bf(w_gla_out[l]), bf(w_mem_out[l]),
                bf(w_out[l]), final_g, batch, seq, n_mem, tm=256)
    return out.reshape(batch, seq, d)
```

```python
import functools

import numpy as np
import jax
import jax.numpy as jnp
from jax import lax
from jax.experimental import pallas as pl
from jax.experimental.pallas import tpu as pltpu

F32 = jnp.float32
BF16 = jnp.bfloat16

D_MODEL = 2048
EPS = 1e-6
CONV_K = 31
GLA_HEADS = 4
GLA_DK = 1024
GLA_DV = 2048
GLA_DKH = GLA_DK // GLA_HEADS
GLA_DVH = GLA_DV // GLA_HEADS
GLA_RANK = 16
GLA_TAU = 16.0
MEM_HEADS = 4
MEM_HD = 128
MEM_WIDTH = MEM_HEADS * MEM_HD
N_BRANCH = 3

LANES = 128
VMEM_LIMIT = 56 * 1024 * 1024

GLA_CHUNK = 256
GLA_LEVELS = 8
RANK_PAD = LANES
CONV_HALO = 32
CONV_TB = 8
LOG2_E = 1.4426950408889634


def _params(sem):
    return pltpu.CompilerParams(dimension_semantics=sem, vmem_limit_bytes=VMEM_LIMIT)


def _fixed(shape, index=None):
    index = (0,) * len(shape) if index is None else tuple(index)
    return pl.BlockSpec(shape, lambda *_: index, pipeline_mode=pl.Buffered(1))


def _sigmoid(x):
    return 1.0 / (1.0 + jnp.exp(-x))


def _silu(x):
    return x * _sigmoid(x)


def _dot(a, b):
    return jnp.dot(a, b, preferred_element_type=F32)


def _dot_nt(a, b):
    return lax.dot_general(a, b, (((1,), (1,)), ((), ())), preferred_element_type=F32)


def _dot_tn(a, b):
    return lax.dot_general(a, b, (((0,), (0,)), ((), ())), preferred_element_type=F32)


def _split_bf16(x):
    hi = x.astype(BF16)
    return hi, (x - hi.astype(F32)).astype(BF16)


def _rms_scale(x):
    return lax.rsqrt(jnp.mean(x * x, axis=-1, keepdims=True) + EPS)


def _rmsnorm_kernel(x_ref, g_ref, o_ref):
    x = x_ref[...]
    o_ref[...] = (x * _rms_scale(x) * g_ref[...]).astype(o_ref.dtype)


def _rmsnorm_bf16(x, g, tm):
    t, d = x.shape
    return pl.pallas_call(
        _rmsnorm_kernel,
        grid=(t // tm,),
        in_specs=[pl.BlockSpec((tm, d), lambda i: (i, 0)), _fixed((1, d))],
        out_specs=pl.BlockSpec((tm, d), lambda i: (i, 0)),
        out_shape=jax.ShapeDtypeStruct((t, d), BF16),
        compiler_params=_params(("parallel",)),
        name="rmsnorm_in",
    )(x, g.reshape(1, d))


def _proj_conv_kernel(h_ref, wa_ref, wb_ref, wz_ref, u_ref, sz_ref):
    h = h_ref[...]
    a = _dot(h, wa_ref[...])
    b = _dot(h, wb_ref[...])
    u_ref[...] = (a * _sigmoid(b)).astype(BF16)
    z = _dot(h, wz_ref[...])
    sz_ref[...] = _silu(z).astype(BF16)


def _proj_vz_kernel(h_ref, wv_ref, wz_ref, v_ref, sz_ref):
    h = h_ref[...]
    v_ref[...] = _dot(h, wv_ref[...]).astype(BF16)
    sz_ref[...] = _silu(_dot(h, wz_ref[...])).astype(BF16)


def _proj_gate_kernel(h_ref, w_ref, b_ref, g_ref):
    g_ref[...] = _sigmoid(_dot(h_ref[...], w_ref[...]) + b_ref[...]).astype(BF16)


def _proj_call(kernel, name, h, w, col_offsets, width, out_dtypes, tm, tn, bias=None):
    t, d = h.shape
    assert all(off % tn == 0 for off in col_offsets) and width % tn == 0
    in_specs = [pl.BlockSpec((tm, d), lambda i, j: (i, 0))]
    in_specs += [pl.BlockSpec((d, tn), lambda i, j, o=off // tn: (0, o + j)) for off in col_offsets]
    args = [h] + [w] * len(col_offsets)
    if bias is not None:
        in_specs.append(pl.BlockSpec((1, tn), lambda i, j: (0, j)))
        args.append(bias.reshape(1, width))
    return pl.pallas_call(
        kernel,
        grid=(t // tm, width // tn),
        in_specs=in_specs,
        out_specs=[pl.BlockSpec((tm, tn), lambda i, j: (i, j)) for _ in out_dtypes],
        out_shape=[jax.ShapeDtypeStruct((t, width), dt) for dt in out_dtypes],
        compiler_params=_params(("parallel", "arbitrary")),
        name=name,
    )(*args)


def _proj_qkm_kernel(h_ref, wq_ref, wk_ref, wmq_ref, wmz_ref, wal_ref, w2_ref, ba_ref,
                     q_ref, k_ref, mq_ref, sz_ref, la_ref):
    h = h_ref[...]
    al_hi, al_lo = _split_bf16(_dot(h, wal_ref[...]))
    w2_hi, w2_lo = _split_bf16(w2_ref[...])
    x = _dot(al_hi, w2_hi) + _dot(al_lo, w2_hi) + _dot(al_hi, w2_lo) + ba_ref[...]
    log_a = (jnp.minimum(x, 0.0) - jnp.log1p(jnp.exp(-jnp.abs(x)))) * (1.0 / GLA_TAU)
    la_ref[...] = (log_a * LOG2_E).astype(BF16)
    q_ref[...] = (_dot(h, wq_ref[...]) * (GLA_DKH ** -0.5)).astype(BF16)
    k_ref[...] = _dot(h, wk_ref[...]).astype(BF16)
    mq_ref[...] = _dot(h, wmq_ref[...]).astype(BF16)
    sz_ref[...] = _silu(_dot(h, wmz_ref[...])).astype(BF16)


def _proj_qkm(h, w, q_off, k_off, w_mq, w_mz, w_al_pad, w2_pad, b_alpha, tm):
    t, d = h.shape
    row = lambda i: (i, 0)
    out = lambda n: pl.BlockSpec((tm, n), row)
    return pl.pallas_call(
        _proj_qkm_kernel,
        grid=(t // tm,),
        in_specs=[pl.BlockSpec((tm, d), row),
                  _fixed((d, GLA_DK), (0, q_off // GLA_DK)), _fixed((d, GLA_DK), (0, k_off // GLA_DK)),
                  _fixed((d, MEM_WIDTH)), _fixed((d, MEM_WIDTH)), _fixed((d, RANK_PAD)),
                  _fixed((RANK_PAD, GLA_DK)), _fixed((1, GLA_DK))],
        out_specs=[out(GLA_DK), out(GLA_DK), out(MEM_WIDTH), out(MEM_WIDTH), out(GLA_DK)],
        out_shape=[jax.ShapeDtypeStruct((t, GLA_DK), BF16), jax.ShapeDtypeStruct((t, GLA_DK), BF16),
                   jax.ShapeDtypeStruct((t, MEM_WIDTH), BF16),
                   jax.ShapeDtypeStruct((t, MEM_WIDTH), BF16),
                   jax.ShapeDtypeStruct((t, GLA_DK), BF16)],
        compiler_params=_params(("parallel",)),
        name="proj_qkm",
    )(h, w, w, w_mq, w_mz, w_al_pad, w2_pad, b_alpha.reshape(1, GLA_DK))


def _dwconv_kernel(u_ref, w_ref, b_ref, o_ref, ext_ref, acc_ref, *, ts):
    s = pl.program_id(1)
    sub = ext_ref.shape[1]

    @pl.when(s == 0)
    def _():
        ext_ref[0:CONV_HALO] = jnp.zeros((CONV_HALO,) + ext_ref.shape[1:], F32)

    @pl.when(s > 0)
    def _():
        ext_ref[0:CONV_HALO] = ext_ref[ts:ts + CONV_HALO]

    ext_ref[CONV_HALO:CONV_HALO + ts] = pltpu.einshape(
        "t(sl)->tsl", u_ref[...].astype(F32), s=sub)
    lead = CONV_HALO - (CONV_K - 1)

    def body(i, carry):
        t0 = pl.multiple_of(i * CONV_TB, CONV_TB)
        acc = jnp.broadcast_to(b_ref[...], (CONV_TB,) + ext_ref.shape[1:])
        for j in range(CONV_K):
            acc = acc + w_ref[j] * ext_ref[pl.ds(t0 + lead + j, CONV_TB)]
        acc_ref[pl.ds(t0, CONV_TB)] = acc
        return carry

    lax.fori_loop(0, ts // CONV_TB, body, 0)
    o_ref[...] = pltpu.einshape("tsl->t(sl)", acc_ref[...]).astype(o_ref.dtype)


def _dwconv(u, w, b, batch, seq, ts):
    t, c = u.shape
    sub = c // LANES
    n_s = seq // ts
    tok = lambda bi, si: (bi * n_s + si, 0)
    return pl.pallas_call(
        functools.partial(_dwconv_kernel, ts=ts),
        grid=(batch, n_s),
        in_specs=[pl.BlockSpec((ts, c), tok), _fixed((CONV_K, sub, LANES)), _fixed((1, sub, LANES))],
        out_specs=pl.BlockSpec((ts, c), tok),
        out_shape=jax.ShapeDtypeStruct((t, c), BF16),
        scratch_shapes=[pltpu.VMEM((ts + CONV_HALO, sub, LANES), F32),
                        pltpu.VMEM((ts, sub, LANES), F32)],
        compiler_params=_params(("arbitrary", "arbitrary")),
        name="dwconv",
    )(u, w.reshape(CONV_K, sub, LANES), b.reshape(1, sub, LANES))


def _gla_constants():
    c = GLA_CHUNK
    t = np.arange(c)[:, None]
    u = np.arange(c)[None, :]
    mats = np.zeros((GLA_LEVELS + 1, c, c), np.float32)
    mats[0] = u <= t
    for lvl in range(GLA_LEVELS):
        m = 1 << lvl
        pos = t % (2 * m)
        r = t - pos + m - 1
        mats[1 + lvl] = ((pos >= m) & (u > r) & (u <= t)) | ((pos < m) & (u > t) & (u <= r))
    x = t ^ u
    lv = np.where(x > 0, np.floor(np.log2(np.maximum(x, 1))).astype(np.int32), GLA_LEVELS)
    lv = np.where(u > t, -1, lv).astype(np.int32)
    return mats.reshape((GLA_LEVELS + 1) * c, c), lv


def _gla_kernel(q_ref, k_ref, v_ref, la_ref, gz_ref, ng_ref, mats_ref, lv_ref, o_ref, st_ref):
    c = GLA_CHUNK

    @pl.when(pl.program_id(1) == 0)
    def _():
        st_ref[...] = jnp.zeros_like(st_ref)

    lv = lv_ref[...]
    for hd in range(GLA_HEADS):
        ks = slice(hd * GLA_DKH, (hd + 1) * GLA_DKH)
        vs = slice(hd * GLA_DVH, (hd + 1) * GLA_DVH)
        q = q_ref[:, ks]
        k = k_ref[:, ks]
        v = v_ref[:, vs]
        qf = q.astype(F32)
        kf = k.astype(F32)

        e_all = _dot(mats_ref[...], la_ref[:, ks])
        b2 = e_all[:c]
        b2_last = b2[c - 1:c, :]
        q_in = (qf * jnp.exp2(b2)).astype(BF16)
        k_up = (kf * jnp.exp2(b2_last - b2)).astype(BF16)

        st = st_ref[hd]
        o = _dot_nt(q_in, st.astype(BF16))

        p = jnp.where(lv == GLA_LEVELS, _dot_nt(q, k), 0.0)
        for lvl in range(GLA_LEVELS):
            w = jnp.exp2(e_all[(1 + lvl) * c:(2 + lvl) * c])
            ql = (qf * w).astype(BF16)
            kl = (kf * w).astype(BF16)
            p = jnp.where(lv == lvl, _dot_nt(ql, kl), p)
        o = o + _dot(p.astype(BF16), v)

        st_ref[hd] = st * jnp.exp2(b2_last) + _dot_tn(v, k_up)

        y = o * _rms_scale(o) * ng_ref[:, vs]
        o_ref[:, vs] = (y * gz_ref[:, vs].astype(F32)).astype(o_ref.dtype)


def _gla(q, k, v, la2, gz, norm_g, batch, seq):
    t = q.shape[0]
    c = GLA_CHUNK
    n_c = seq // c
    mats, lv = _gla_constants()
    tok = lambda bi, ni: (bi * n_c + ni, 0)
    return pl.pallas_call(
        _gla_kernel,
        grid=(batch, n_c),
        in_specs=[pl.BlockSpec((c, GLA_DK), tok), pl.BlockSpec((c, GLA_DK), tok),
                  pl.BlockSpec((c, GLA_DV), tok), pl.BlockSpec((c, GLA_DK), tok),
                  pl.BlockSpec((c, GLA_DV), tok), _fixed((1, GLA_DV)),
                  _fixed(mats.shape), _fixed(lv.shape)],
        out_specs=pl.BlockSpec((c, GLA_DV), tok),
        out_shape=jax.ShapeDtypeStruct((t, GLA_DV), BF16),
        scratch_shapes=[pltpu.VMEM((GLA_HEADS, GLA_DVH, GLA_DKH), F32)],
        compiler_params=_params(("arbitrary", "arbitrary")),
        name="gla",
    )(q, k, v, la2, gz, norm_g.reshape(1, GLA_DV), jnp.asarray(mats, BF16), jnp.asarray(lv))


def _mem_kv_kernel(m_ref, g_ref, w_ref, o_ref):
    m = m_ref[...]
    mn = (m * _rms_scale(m) * g_ref[...]).astype(BF16)
    o_ref[...] = _dot(mn, w_ref[...]).astype(o_ref.dtype)


def _mem_kv(mem2, g, w, batch, n_mem):
    d = mem2.shape[1]
    n = w.shape[1]
    return pl.pallas_call(
        _mem_kv_kernel,
        grid=(batch,),
        in_specs=[pl.BlockSpec((n_mem, d), lambda bi: (bi, 0)), _fixed((1, d)), _fixed((d, n))],
        out_specs=pl.BlockSpec((n_mem, n), lambda bi: (bi, 0)),
        out_shape=jax.ShapeDtypeStruct((batch * n_mem, n), BF16),
        compiler_params=_params(("parallel",)),
        name="mem_kv",
    )(mem2, g.reshape(1, d), w)


def _tail_kernel(c_ref, csz_ref, og_ref, mq_ref, msz_ref, kv_ref, g_ref, x_ref,
                 lg_ref, lb_ref, wc_ref, bc_ref, wg_ref, wm_ref, wo_ref, fg_ref, o_ref):
    d = D_MODEL
    c = c_ref[...].astype(F32)
    dev = c - jnp.mean(c, axis=-1, keepdims=True)
    var = jnp.mean(dev * dev, axis=-1, keepdims=True)
    yh = dev * lax.rsqrt(var + EPS) * lg_ref[...] + lb_ref[...]
    u = _silu(yh) * csz_ref[...].astype(F32)
    y = (_dot(u.astype(BF16), wc_ref[...]) + bc_ref[...]) * g_ref[:, 0:d].astype(F32)

    y = y + _dot(og_ref[...], wg_ref[...]) * g_ref[:, d:2 * d].astype(F32)

    q = mq_ref[...]
    kv = kv_ref[...]
    outs = []
    for hd in range(MEM_HEADS):
        lo, hi = hd * MEM_HD, (hd + 1) * MEM_HD
        sc = _dot_nt(q[:, lo:hi], kv[:, lo:hi]) * (MEM_HD ** -0.5)
        e = jnp.exp(sc - jnp.max(sc, axis=-1, keepdims=True))
        p = e / jnp.sum(e, axis=-1, keepdims=True)
        outs.append(_dot(p.astype(BF16), kv[:, MEM_WIDTH + lo:MEM_WIDTH + hi]))
    om = jnp.concatenate(outs, axis=-1) * msz_ref[...].astype(F32)
    y = y + _dot(om.astype(BF16), wm_ref[...]) * g_ref[:, 2 * d:3 * d].astype(F32)

    r = x_ref[...] + _dot(y.astype(BF16), wo_ref[...])
    o_ref[...] = r * _rms_scale(r) * fg_ref[...]


def _tail(conv, conv_sz, og, mq, mem_sz, mkv, gates, x, ln_g, ln_b, w_conv, b_conv, w_gla,
          w_mem, w_out, final_g, batch, seq, n_mem, tm):
    t, d = x.shape
    n_t = seq // tm
    tok = lambda bi, ti: (bi * n_t + ti, 0)
    wide = lambda w: pl.BlockSpec((tm, w), tok)
    return pl.pallas_call(
        _tail_kernel,
        grid=(batch, n_t),
        in_specs=[wide(d), wide(d), wide(GLA_DV), wide(MEM_WIDTH), wide(MEM_WIDTH),
                  pl.BlockSpec((n_mem, 2 * MEM_WIDTH), lambda bi, ti: (bi, 0)),
                  wide(N_BRANCH * d), wide(d),
                  _fixed((1, d)), _fixed((1, d)), _fixed((d, d)), _fixed((1, d)),
                  _fixed((GLA_DV, d)), _fixed((MEM_WIDTH, d)), _fixed((d, d)), _fixed((1, d))],
        out_specs=wide(d),
        out_shape=jax.ShapeDtypeStruct((t, d), F32),
        compiler_params=_params(("parallel", "parallel")),
        name="tail",
    )(conv, conv_sz, og, mq, mem_sz, mkv, gates, x, ln_g.reshape(1, d), ln_b.reshape(1, d),
      w_conv, b_conv.reshape(1, d), w_gla, w_mem, w_out, final_g.reshape(1, d))


def kernel(x, mem, ln_in_g, mem_ln_g, w_in, b_gate, dw_w, dw_b, conv_ln_g, conv_ln_b,
           w_conv_out, b_conv_out, w_alpha2, b_alpha, gla_norm_g, w_gla_out,
           w_mem_kv, w_mem_out, w_out, final_g):
    batch, seq, d = x.shape
    n_mem = mem.shape[1]
    depth = w_in.shape[0]
    assert depth == 1 and d == D_MODEL
    t = batch * seq
    xs = x.reshape(t, d)
    mem2 = mem.reshape(batch * n_mem, d)
    bf = lambda a: a.astype(BF16)
    l = 0

    widths = [d, d, d, GLA_DK, GLA_DK, GLA_DV, GLA_DV, GLA_RANK, MEM_WIDTH, MEM_WIDTH, N_BRANCH * d]
    offs = [int(o) for o in np.cumsum([0] + widths)]
    (o_ca, o_cb, o_cz, o_q, o_k, o_v, o_gz, o_al, o_mq, o_mz, o_g, o_end) = offs
    w = w_in[l]
    w_main = bf(w[:, :o_al])
    w_al_pad = jnp.zeros((d, RANK_PAD), BF16).at[:, :GLA_RANK].set(bf(w[:, o_al:o_mq]))
    w_mq, w_mz, w_g = bf(w[:, o_mq:o_mz]), bf(w[:, o_mz:o_g]), bf(w[:, o_g:o_end])
    w2_pad = jnp.zeros((RANK_PAD, GLA_DK), F32).at[:GLA_RANK].set(w_alpha2[l])

    h = _rmsnorm_bf16(xs, ln_in_g[l], tm=512)

    u, conv_sz = _proj_call(_proj_conv_kernel, "proj_conv", h, w_main, [o_ca, o_cb, o_cz], d,
                            [BF16, BF16], tm=1024, tn=512)
    q, k, mq, mem_sz, la2 = _proj_qkm(h, w_main, o_q, o_k, w_mq, w_mz, w_al_pad, w2_pad,
                                      b_alpha[l], tm=512)
    v, gla_sz = _proj_call(_proj_vz_kernel, "proj_vz", h, w_main, [o_v, o_gz], GLA_DV,
                           [BF16, BF16], tm=1024, tn=1024)
    (gates,) = _proj_call(_proj_gate_kernel, "proj_gate", h, w_g, [0], N_BRANCH * d, [BF16],
                          tm=1024, tn=2048, bias=b_gate[l])

    conv = _dwconv(u, dw_w[l], dw_b[l], batch, seq, ts=512)
    og = _gla(q, k, v, la2, gla_sz, gla_norm_g[l], batch, seq)
    mkv = _mem_kv(mem2, mem_ln_g[l], bf(w_mem_kv[l]), batch, n_mem)

    out = _tail(conv, conv_sz, og, mq, mem_sz, mkv, gates, xs, conv_ln_g[l], conv_ln_b[l],
                bf(w_conv_out[l]), b_conv_out[l], bf(w_gla_out[l]), bf(w_mem_out[l]),
                bf(w_out[l]), final_g, batch, seq, n_mem, tm=256)
    return out.reshape(batch, seq, d)
```

```python
import functools

import numpy as np
import jax
import jax.numpy as jnp
from jax import lax
from jax.experimental import pallas as pl
from jax.experimental.pallas import tpu as pltpu

F32 = jnp.float32
BF16 = jnp.bfloat16

D_MODEL = 2048
EPS = 1e-6
CONV_K = 31
GLA_HEADS = 4
GLA_DK = 1024
GLA_DV = 2048
GLA_DKH = GLA_DK // GLA_HEADS
GLA_DVH = GLA_DV // GLA_HEADS
GLA_RANK = 16
GLA_TAU = 16.0
MEM_HEADS = 4
MEM_HD = 128
MEM_WIDTH = MEM_HEADS * MEM_HD
N_BRANCH = 3

LANES = 128
VMEM_LIMIT = 60 * 1024 * 1024

GLA_CHUNK = 256
GLA_LEVELS = 8
RANK_PAD = LANES
CONV_HALO = 32
CONV_TB = 8
LOG2_E = 1.4426950408889634
W_ROW_ALIGN = 16


def _params(sem):
    return pltpu.CompilerParams(dimension_semantics=sem, vmem_limit_bytes=VMEM_LIMIT)


def _fixed(shape, index=None):
    index = (0,) * len(shape) if index is None else tuple(index)
    return pl.BlockSpec(shape, lambda *_: index, pipeline_mode=pl.Buffered(1))


def _sigmoid(x):
    return 1.0 / (1.0 + jnp.exp(-x))


def _silu(x):
    return x * _sigmoid(x)


def _dot(a, b):
    return jnp.dot(a, b, preferred_element_type=F32)


def _dot_nt(a, b):
    return lax.dot_general(a, b, (((1,), (1,)), ((), ())), preferred_element_type=F32)


def _dot_tn(a, b):
    return lax.dot_general(a, b, (((0,), (0,)), ((), ())), preferred_element_type=F32)


def _split_bf16(x):
    hi = x.astype(BF16)
    return hi, (x - hi.astype(F32)).astype(BF16)


def _rms_scale(x):
    return lax.rsqrt(jnp.mean(x * x, axis=-1, keepdims=True) + EPS)


def _rmsnorm_kernel(x_ref, g_ref, o_ref):
    x = x_ref[...]
    o_ref[...] = (x * _rms_scale(x) * g_ref[...]).astype(o_ref.dtype)


def _rmsnorm_bf16(x, g, tm):
    t, d = x.shape
    return pl.pallas_call(
        _rmsnorm_kernel,
        grid=(t // tm,),
        in_specs=[pl.BlockSpec((tm, d), lambda i: (i, 0)), _fixed((1, d))],
        out_specs=pl.BlockSpec((tm, d), lambda i: (i, 0)),
        out_shape=jax.ShapeDtypeStruct((t, d), BF16),
        compiler_params=_params(("parallel",)),
        name="rmsnorm_in",
    )(x, g.reshape(1, d))


def _proj_w(h, wt_ref):
    return _dot_nt(h, wt_ref[...].astype(BF16))


def _proj_conv_kernel(h_ref, wa_ref, wb_ref, wz_ref, u_ref, sz_ref):
    h = h_ref[...]
    u_ref[...] = (_proj_w(h, wa_ref) * _sigmoid(_proj_w(h, wb_ref))).astype(BF16)
    sz_ref[...] = _silu(_proj_w(h, wz_ref)).astype(BF16)


def _proj_vz_kernel(h_ref, wv_ref, wz_ref, v_ref, sz_ref):
    h = h_ref[...]
    v_ref[...] = _proj_w(h, wv_ref).astype(BF16)
    sz_ref[...] = _silu(_proj_w(h, wz_ref)).astype(BF16)


def _proj_gate_kernel(h_ref, w_ref, b_ref, g_ref):
    g_ref[...] = _sigmoid(_proj_w(h_ref[...], w_ref) + b_ref[...]).astype(BF16)


def _wt_rows(rows, d, index_map):
    def im(*idx):
        return pl.multiple_of(index_map(*idx), W_ROW_ALIGN), 0
    return pl.BlockSpec((pl.Element(rows), pl.Element(d)), im)


def _proj_call(kernel, name, h, wt, row_offsets, width, out_dtypes, tm, tn, bias=None):
    t, d = h.shape
    assert all(off % W_ROW_ALIGN == 0 for off in row_offsets) and width % tn == 0
    in_specs = [pl.BlockSpec((tm, d), lambda i, j: (i, 0))]
    in_specs += [_wt_rows(tn, d, lambda i, j, o=off: o + j * tn) for off in row_offsets]
    args = [h] + [wt] * len(row_offsets)
    if bias is not None:
        in_specs.append(pl.BlockSpec((1, tn), lambda i, j: (0, j)))
        args.append(bias.reshape(1, width))
    return pl.pallas_call(
        kernel,
        grid=(t // tm, width // tn),
        in_specs=in_specs,
        out_specs=[pl.BlockSpec((tm, tn), lambda i, j: (i, j)) for _ in out_dtypes],
        out_shape=[jax.ShapeDtypeStruct((t, width), dt) for dt in out_dtypes],
        compiler_params=_params(("parallel", "arbitrary")),
        name=name,
    )(*args)


def _proj_qkm_kernel(h_ref, wq_ref, wk_ref, wmq_ref, wmz_ref, wal_ref, w2_ref, ba_ref,
                     q_ref, k_ref, mq_ref, sz_ref, la_ref):
    h = h_ref[...]
    al_hi, al_lo = _split_bf16(_proj_w(h, wal_ref))
    w2_hi, w2_lo = _split_bf16(w2_ref[...])
    x = _dot(al_hi, w2_hi) + _dot(al_lo, w2_hi) + _dot(al_hi, w2_lo) + ba_ref[...]
    log_a = (jnp.minimum(x, 0.0) - jnp.log1p(jnp.exp(-jnp.abs(x)))) * (1.0 / GLA_TAU)
    la_ref[...] = (log_a * LOG2_E).astype(BF16)
    q_ref[...] = (_proj_w(h, wq_ref) * (GLA_DKH ** -0.5)).astype(BF16)
    k_ref[...] = _proj_w(h, wk_ref).astype(BF16)
    mq_ref[...] = _proj_w(h, wmq_ref).astype(BF16)
    sz_ref[...] = _silu(_proj_w(h, wmz_ref)).astype(BF16)


def _proj_qkm(h, wt, q_off, k_off, al_off, mq_off, mz_off, w2_pad, b_alpha, tm):
    t, d = h.shape
    row = lambda i: (i, 0)
    out = lambda n: pl.BlockSpec((tm, n), row)
    fixed_rows = lambda rows, off: pl.BlockSpec(
        (pl.Element(rows), pl.Element(d)), lambda i: (off, 0), pipeline_mode=pl.Buffered(1))
    assert all(off % W_ROW_ALIGN == 0 for off in (q_off, k_off, al_off, mq_off, mz_off))
    return pl.pallas_call(
        _proj_qkm_kernel,
        grid=(t // tm,),
        in_specs=[pl.BlockSpec((tm, d), row),
                  fixed_rows(GLA_DK, q_off), fixed_rows(GLA_DK, k_off),
                  fixed_rows(MEM_WIDTH, mq_off), fixed_rows(MEM_WIDTH, mz_off),
                  fixed_rows(RANK_PAD, al_off),
                  _fixed((RANK_PAD, GLA_DK)), _fixed((1, GLA_DK))],
        out_specs=[out(GLA_DK), out(GLA_DK), out(MEM_WIDTH), out(MEM_WIDTH), out(GLA_DK)],
        out_shape=[jax.ShapeDtypeStruct((t, GLA_DK), BF16), jax.ShapeDtypeStruct((t, GLA_DK), BF16),
                   jax.ShapeDtypeStruct((t, MEM_WIDTH), BF16),
                   jax.ShapeDtypeStruct((t, MEM_WIDTH), BF16),
                   jax.ShapeDtypeStruct((t, GLA_DK), BF16)],
        compiler_params=_params(("parallel",)),
        name="proj_qkm",
    )(h, wt, wt, wt, wt, wt, w2_pad, b_alpha.reshape(1, GLA_DK))


def _dwconv_kernel(u_ref, w_ref, b_ref, o_ref, ext_ref, acc_ref, *, ts):
    s = pl.program_id(1)
    sub = ext_ref.shape[1]

    @pl.when(s == 0)
    def _():
        ext_ref[0:CONV_HALO] = jnp.zeros((CONV_HALO,) + ext_ref.shape[1:], F32)

    @pl.when(s > 0)
    def _():
        ext_ref[0:CONV_HALO] = ext_ref[ts:ts + CONV_HALO]

    ext_ref[CONV_HALO:CONV_HALO + ts] = pltpu.einshape(
        "t(sl)->tsl", u_ref[...].astype(F32), s=sub)
    lead = CONV_HALO - (CONV_K - 1)

    def body(i, carry):
        t0 = pl.multiple_of(i * CONV_TB, CONV_TB)
        acc = jnp.broadcast_to(b_ref[...], (CONV_TB,) + ext_ref.shape[1:])
        for j in range(CONV_K):
            acc = acc + w_ref[j] * ext_ref[pl.ds(t0 + lead + j, CONV_TB)]
        acc_ref[pl.ds(t0, CONV_TB)] = acc
        return carry

    lax.fori_loop(0, ts // CONV_TB, body, 0)
    o_ref[...] = pltpu.einshape("tsl->t(sl)", acc_ref[...]).astype(o_ref.dtype)


def _dwconv(u, w, b, batch, seq, ts):
    t, c = u.shape
    sub = c // LANES
    n_s = seq // ts
    tok = lambda bi, si: (bi * n_s + si, 0)
    return pl.pallas_call(
        functools.partial(_dwconv_kernel, ts=ts),
        grid=(batch, n_s),
        in_specs=[pl.BlockSpec((ts, c), tok), _fixed((CONV_K, sub, LANES)), _fixed((1, sub, LANES))],
        out_specs=pl.BlockSpec((ts, c), tok),
        out_shape=jax.ShapeDtypeStruct((t, c), BF16),
        scratch_shapes=[pltpu.VMEM((ts + CONV_HALO, sub, LANES), F32),
                        pltpu.VMEM((ts, sub, LANES), F32)],
        compiler_params=_params(("arbitrary", "arbitrary")),
        name="dwconv",
    )(u, w.reshape(CONV_K, sub, LANES), b.reshape(1, sub, LANES))


def _gla_constants():
    c = GLA_CHUNK
    t = np.arange(c)[:, None]
    u = np.arange(c)[None, :]
    mats = np.zeros((GLA_LEVELS + 1, c, c), np.float32)
    mats[0] = u <= t
    for lvl in range(GLA_LEVELS):
        m = 1 << lvl
        pos = t % (2 * m)
        r = t - pos + m - 1
        mats[1 + lvl] = ((pos >= m) & (u > r) & (u <= t)) | ((pos < m) & (u > t) & (u <= r))
    x = t ^ u
    lv = np.where(x > 0, np.floor(np.log2(np.maximum(x, 1))).astype(np.int32), GLA_LEVELS)
    lv = np.where(u > t, -1, lv).astype(np.int32)
    return mats.reshape((GLA_LEVELS + 1) * c, c), lv


def _gla_kernel(q_ref, k_ref, v_ref, la_ref, gz_ref, ng_ref, mats_ref, lv_ref, o_ref, st_ref):
    c = GLA_CHUNK

    @pl.when(pl.program_id(1) == 0)
    def _():
        st_ref[...] = jnp.zeros_like(st_ref)

    lv = lv_ref[...]
    for hd in range(GLA_HEADS):
        ks = slice(hd * GLA_DKH, (hd + 1) * GLA_DKH)
        vs = slice(hd * GLA_DVH, (hd + 1) * GLA_DVH)
        q = q_ref[:, ks]
        k = k_ref[:, ks]
        v = v_ref[:, vs]
        qf = q.astype(F32)
        kf = k.astype(F32)

        e_all = _dot(mats_ref[...], la_ref[:, ks])
        b2 = e_all[:c]
        b2_last = b2[c - 1:c, :]
        q_in = (qf * jnp.exp2(b2)).astype(BF16)
        k_up = (kf * jnp.exp2(b2_last - b2)).astype(BF16)

        st = st_ref[hd]
        o = _dot_nt(q_in, st.astype(BF16))

        p = jnp.where(lv == GLA_LEVELS, _dot_nt(q, k), 0.0)
        for lvl in range(GLA_LEVELS):
            w = jnp.exp2(e_all[(1 + lvl) * c:(2 + lvl) * c])
            ql = (qf * w).astype(BF16)
            kl = (kf * w).astype(BF16)
            p = jnp.where(lv == lvl, _dot_nt(ql, kl), p)
        o = o + _dot(p.astype(BF16), v)

        st_ref[hd] = st * jnp.exp2(b2_last) + _dot_tn(v, k_up)

        y = o * _rms_scale(o) * ng_ref[:, vs]
        o_ref[:, vs] = (y * gz_ref[:, vs].astype(F32)).astype(o_ref.dtype)


def _gla(q, k, v, la2, gz, norm_g, batch, seq):
    t = q.shape[0]
    c = GLA_CHUNK
    n_c = seq // c
    mats, lv = _gla_constants()
    tok = lambda bi, ni: (bi * n_c + ni, 0)
    return pl.pallas_call(
        _gla_kernel,
        grid=(batch, n_c),
        in_specs=[pl.BlockSpec((c, GLA_DK), tok), pl.BlockSpec((c, GLA_DK), tok),
                  pl.BlockSpec((c, GLA_DV), tok), pl.BlockSpec((c, GLA_DK), tok),
                  pl.BlockSpec((c, GLA_DV), tok), _fixed((1, GLA_DV)),
                  _fixed(mats.shape), _fixed(lv.shape)],
        out_specs=pl.BlockSpec((c, GLA_DV), tok),
        out_shape=jax.ShapeDtypeStruct((t, GLA_DV), BF16),
        scratch_shapes=[pltpu.VMEM((GLA_HEADS, GLA_DVH, GLA_DKH), F32)],
        compiler_params=_params(("arbitrary", "arbitrary")),
        name="gla",
    )(q, k, v, la2, gz, norm_g.reshape(1, GLA_DV), jnp.asarray(mats, BF16), jnp.asarray(lv))


def _mem_kv_kernel(m_ref, g_ref, w_ref, o_ref):
    m = m_ref[...]
    mn = (m * _rms_scale(m) * g_ref[...]).astype(BF16)
    o_ref[...] = _dot(mn, w_ref[...]).astype(o_ref.dtype)


def _mem_kv(mem2, g, w, batch, n_mem):
    d = mem2.shape[1]
    n = w.shape[1]
    return pl.pallas_call(
        _mem_kv_kernel,
        grid=(batch,),
        in_specs=[pl.BlockSpec((n_mem, d), lambda bi: (bi, 0)), _fixed((1, d)), _fixed((d, n))],
        out_specs=pl.BlockSpec((n_mem, n), lambda bi: (bi, 0)),
        out_shape=jax.ShapeDtypeStruct((batch * n_mem, n), BF16),
        compiler_params=_params(("parallel",)),
        name="mem_kv",
    )(mem2, g.reshape(1, d), w)


def _branches_kernel(c_ref, csz_ref, og_ref, mq_ref, msz_ref, kv_ref, g_ref,
                     lg_ref, lb_ref, wc_ref, bc_ref, wg_ref, wm_ref, y_ref):
    d = D_MODEL
    c = c_ref[...].astype(F32)
    dev = c - jnp.mean(c, axis=-1, keepdims=True)
    var = jnp.mean(dev * dev, axis=-1, keepdims=True)
    yh = dev * lax.rsqrt(var + EPS) * lg_ref[...] + lb_ref[...]
    u = _silu(yh) * csz_ref[...].astype(F32)
    y = (_dot(u.astype(BF16), wc_ref[...]) + bc_ref[...]) * g_ref[:, 0:d].astype(F32)

    y = y + _dot(og_ref[...], wg_ref[...]) * g_ref[:, d:2 * d].astype(F32)

    q = mq_ref[...]
    kv = kv_ref[...]
    outs = []
    for hd in range(MEM_HEADS):
        lo, hi = hd * MEM_HD, (hd + 1) * MEM_HD
        sc = _dot_nt(q[:, lo:hi], kv[:, lo:hi]) * (MEM_HD ** -0.5)
        e = jnp.exp(sc - jnp.max(sc, axis=-1, keepdims=True))
        p = e / jnp.sum(e, axis=-1, keepdims=True)
        outs.append(_dot(p.astype(BF16), kv[:, MEM_WIDTH + lo:MEM_WIDTH + hi]))
    om = jnp.concatenate(outs, axis=-1) * msz_ref[...].astype(F32)
    y = y + _dot(om.astype(BF16), wm_ref[...]) * g_ref[:, 2 * d:3 * d].astype(F32)
    y_ref[...] = y.astype(y_ref.dtype)


def _branches(conv, conv_sz, og, mq, mem_sz, mkv, gates, ln_g, ln_b, w_conv, b_conv, w_gla,
              w_mem, batch, seq, n_mem, tm):
    t, d = conv.shape
    n_t = seq // tm
    tok = lambda bi, ti: (bi * n_t + ti, 0)
    wide = lambda w: pl.BlockSpec((tm, w), tok)
    return pl.pallas_call(
        _branches_kernel,
        grid=(batch, n_t),
        in_specs=[wide(d), wide(d), wide(GLA_DV), wide(MEM_WIDTH), wide(MEM_WIDTH),
                  pl.BlockSpec((n_mem, 2 * MEM_WIDTH), lambda bi, ti: (bi, 0)),
                  wide(N_BRANCH * d),
                  _fixed((1, d)), _fixed((1, d)), _fixed((d, d)), _fixed((1, d)),
                  _fixed((GLA_DV, d)), _fixed((MEM_WIDTH, d))],
        out_specs=wide(d),
        out_shape=jax.ShapeDtypeStruct((t, d), BF16),
        compiler_params=_params(("parallel", "parallel")),
        name="branches",
    )(conv, conv_sz, og, mq, mem_sz, mkv, gates, ln_g.reshape(1, d), ln_b.reshape(1, d),
      w_conv, b_conv.reshape(1, d), w_gla, w_mem)


def _out_kernel(y_ref, x_ref, w_ref, g_ref, o_ref):
    r = x_ref[...] + _dot(y_ref[...], w_ref[...])
    o_ref[...] = r * _rms_scale(r) * g_ref[...]


def _out_proj(y, x, w, g, tm):
    t, d = x.shape
    row = lambda i: (i, 0)
    return pl.pallas_call(
        _out_kernel,
        grid=(t // tm,),
        in_specs=[pl.BlockSpec((tm, d), row), pl.BlockSpec((tm, d), row), _fixed((d, d)),
                  _fixed((1, d))],
        out_specs=pl.BlockSpec((tm, d), row),
        out_shape=jax.ShapeDtypeStruct((t, d), F32),
        compiler_params=_params(("parallel",)),
        name="out_proj",
    )(y, x, w, g.reshape(1, d))


def kernel(x, mem, ln_in_g, mem_ln_g, w_in, b_gate, dw_w, dw_b, conv_ln_g, conv_ln_b,
           w_conv_out, b_conv_out, w_alpha2, b_alpha, gla_norm_g, w_gla_out,
           w_mem_kv, w_mem_out, w_out, final_g):
    batch, seq, d = x.shape
    n_mem = mem.shape[1]
    depth = w_in.shape[0]
    assert depth == 1 and d == D_MODEL
    t = batch * seq
    xs = x.reshape(t, d)
    mem2 = mem.reshape(batch * n_mem, d)
    bf = lambda a: a.astype(BF16)
    l = 0

    widths = [d, d, d, GLA_DK, GLA_DK, GLA_DV, GLA_DV, GLA_RANK, MEM_WIDTH, MEM_WIDTH, N_BRANCH * d]
    offs = [int(o) for o in np.cumsum([0] + widths)]
    (o_ca, o_cb, o_cz, o_q, o_k, o_v, o_gz, o_al, o_mq, o_mz, o_g, o_end) = offs
    wt = w_in[l].T
    w2_pad = jnp.zeros((RANK_PAD, GLA_DK), F32).at[:GLA_RANK].set(w_alpha2[l])

    h = _rmsnorm_bf16(xs, ln_in_g[l], tm=512)

    u, conv_sz = _proj_call(_proj_conv_kernel, "proj_conv", h, wt, [o_ca, o_cb, o_cz], d,
                            [BF16, BF16], tm=1024, tn=512)
    q, k, mq, mem_sz, la2 = _proj_qkm(h, wt, o_q, o_k, o_al, o_mq, o_mz, w2_pad, b_alpha[l],
                                      tm=512)
    v, gla_sz = _proj_call(_proj_vz_kernel, "proj_vz", h, wt, [o_v, o_gz], GLA_DV,
                           [BF16, BF16], tm=1024, tn=512)
    (gates,) = _proj_call(_proj_gate_kernel, "proj_gate", h, wt, [o_g], N_BRANCH * d, [BF16],
                          tm=1024, tn=1024, bias=b_gate[l])

    conv = _dwconv(u, dw_w[l], dw_b[l], batch, seq, ts=512)
    og = _gla(q, k, v, la2, gla_sz, gla_norm_g[l], batch, seq)
    mkv = _mem_kv(mem2, mem_ln_g[l], bf(w_mem_kv[l]), batch, n_mem)

    y = _branches(conv, conv_sz, og, mq, mem_sz, mkv, gates, conv_ln_g[l], conv_ln_b[l],
                  bf(w_conv_out[l]), b_conv_out[l], bf(w_gla_out[l]), bf(w_mem_out[l]),
                  batch, seq, n_mem, tm=512)
    out = _out_proj(y, xs, bf(w_out[l]), final_g, tm=512)
    return out.reshape(batch, seq, d)
```

```python
import functools

import numpy as np
import jax
import jax.numpy as jnp
from jax import lax
from jax.experimental import pallas as pl
from jax.experimental.pallas import tpu as pltpu

F32 = jnp.float32
BF16 = jnp.bfloat16

D_MODEL = 2048
EPS = 1e-6
CONV_K = 31
GLA_HEADS = 4
GLA_DK = 1024
GLA_DV = 2048
GLA_DKH = GLA_DK // GLA_HEADS
GLA_DVH = GLA_DV // GLA_HEADS
GLA_RANK = 16
GLA_TAU = 16.0
MEM_HEADS = 4
MEM_HD = 128
MEM_WIDTH = MEM_HEADS * MEM_HD
N_BRANCH = 3

LANES = 128
VMEM_LIMIT = 60 * 1024 * 1024

GLA_CHUNK = 256
GLA_LEVELS = 8
RANK_PAD = LANES
CONV_HALO = 32
CONV_TB = 8
LOG2_E = 1.4426950408889634
W_ROW_ALIGN = 16


def _params(sem):
    return pltpu.CompilerParams(dimension_semantics=sem, vmem_limit_bytes=VMEM_LIMIT)


def _fixed(shape, index=None):
    index = (0,) * len(shape) if index is None else tuple(index)
    return pl.BlockSpec(shape, lambda *_: index, pipeline_mode=pl.Buffered(1))


def _sigmoid(x):
    return 1.0 / (1.0 + jnp.exp(-x))


def _silu(x):
    return x * _sigmoid(x)


def _dot(a, b):
    return jnp.dot(a, b, preferred_element_type=F32)


def _dot_nt(a, b):
    return lax.dot_general(a, b, (((1,), (1,)), ((), ())), preferred_element_type=F32)


def _dot_tn(a, b):
    return lax.dot_general(a, b, (((0,), (0,)), ((), ())), preferred_element_type=F32)


def _split_bf16(x):
    hi = x.astype(BF16)
    return hi, (x - hi.astype(F32)).astype(BF16)


def _rms_scale(x):
    return lax.rsqrt(jnp.mean(x * x, axis=-1, keepdims=True) + EPS)


def _rmsnorm_kernel(x_ref, g_ref, o_ref):
    x = x_ref[...]
    o_ref[...] = (x * _rms_scale(x) * g_ref[...]).astype(o_ref.dtype)


def _rmsnorm_bf16(x, g, tm):
    t, d = x.shape
    return pl.pallas_call(
        _rmsnorm_kernel,
        grid=(t // tm,),
        in_specs=[pl.BlockSpec((tm, d), lambda i: (i, 0)), _fixed((1, d))],
        out_specs=pl.BlockSpec((tm, d), lambda i: (i, 0)),
        out_shape=jax.ShapeDtypeStruct((t, d), BF16),
        compiler_params=_params(("parallel",)),
        name="rmsnorm_in",
    )(x, g.reshape(1, d))


def _proj_w(h, wt_ref):
    return _dot_nt(h, wt_ref[...].astype(BF16))


def _round_weights_once(wt_refs, wb_ref):
    @pl.when(pl.program_id(1) == 0)
    def _():
        for n, wt_ref in enumerate(wt_refs):
            wb_ref[n] = wt_ref[...].astype(BF16)


def _proj_conv_kernel(h_ref, wa_ref, wb_ref, wz_ref, u_ref, sz_ref, w_ref):
    _round_weights_once((wa_ref, wb_ref, wz_ref), w_ref)
    h = h_ref[...]
    u_ref[...] = (_dot_nt(h, w_ref[0]) * _sigmoid(_dot_nt(h, w_ref[1]))).astype(BF16)
    sz_ref[...] = _silu(_dot_nt(h, w_ref[2])).astype(BF16)


def _proj_vz_kernel(h_ref, wv_ref, wz_ref, v_ref, sz_ref, w_ref):
    _round_weights_once((wv_ref, wz_ref), w_ref)
    h = h_ref[...]
    v_ref[...] = _dot_nt(h, w_ref[0]).astype(BF16)
    sz_ref[...] = _silu(_dot_nt(h, w_ref[1])).astype(BF16)


def _proj_gate_kernel(h_ref, wg_ref, b_ref, g_ref, w_ref):
    _round_weights_once((wg_ref,), w_ref)
    g_ref[...] = _sigmoid(_dot_nt(h_ref[...], w_ref[0]) + b_ref[...]).astype(BF16)


def _wt_rows(rows, d, index_map):
    def im(*idx):
        return pl.multiple_of(index_map(*idx), W_ROW_ALIGN), 0
    return pl.BlockSpec((pl.Element(rows), pl.Element(d)), im)


def _proj_call(kernel, name, h, wt, row_offsets, width, out_dtypes, tm, tn, bias=None):
    t, d = h.shape
    assert all(off % W_ROW_ALIGN == 0 for off in row_offsets) and width % tn == 0
    in_specs = [pl.BlockSpec((tm, d), lambda j, i: (i, 0))]
    in_specs += [_wt_rows(tn, d, lambda j, i, o=off: o + j * tn) for off in row_offsets]
    args = [h] + [wt] * len(row_offsets)
    if bias is not None:
        in_specs.append(pl.BlockSpec((1, tn), lambda j, i: (0, j)))
        args.append(bias.reshape(1, width))
    return pl.pallas_call(
        kernel,
        grid=(width // tn, t // tm),
        in_specs=in_specs,
        out_specs=[pl.BlockSpec((tm, tn), lambda j, i: (i, j)) for _ in out_dtypes],
        out_shape=[jax.ShapeDtypeStruct((t, width), dt) for dt in out_dtypes],
        scratch_shapes=[pltpu.VMEM((len(row_offsets), tn, d), BF16)],
        compiler_params=_params(("arbitrary", "arbitrary")),
        name=name,
    )(*args)


def _log_decay_kernel(al_ref, w2_ref, ba_ref, la_ref):
    al_hi, al_lo = _split_bf16(al_ref[...])
    w2_hi, w2_lo = _split_bf16(w2_ref[...])
    x = _dot(al_hi, w2_hi) + _dot(al_lo, w2_hi) + _dot(al_hi, w2_lo) + ba_ref[...]
    log_a = (jnp.minimum(x, 0.0) - jnp.log1p(jnp.exp(-jnp.abs(x)))) * (1.0 / GLA_TAU)
    la_ref[...] = (log_a * LOG2_E).astype(BF16)


def _log_decay(alpha, w2_pad, b_alpha, tm):
    t = alpha.shape[0]
    return pl.pallas_call(
        _log_decay_kernel,
        grid=(t // tm,),
        in_specs=[pl.BlockSpec((tm, RANK_PAD), lambda i: (i, 0)), _fixed((RANK_PAD, GLA_DK)),
                  _fixed((1, GLA_DK))],
        out_specs=pl.BlockSpec((tm, GLA_DK), lambda i: (i, 0)),
        out_shape=jax.ShapeDtypeStruct((t, GLA_DK), BF16),
        compiler_params=_params(("parallel",)),
        name="log_decay",
    )(alpha, w2_pad, b_alpha.reshape(1, GLA_DK))


def _proj_qkm_kernel(h_ref, wq_ref, wk_ref, wmq_ref, wmz_ref, wal_ref,
                     q_ref, k_ref, mq_ref, sz_ref, al_ref):
    h = h_ref[...]
    al_ref[...] = _proj_w(h, wal_ref)
    q_ref[...] = (_proj_w(h, wq_ref) * (GLA_DKH ** -0.5)).astype(BF16)
    k_ref[...] = _proj_w(h, wk_ref).astype(BF16)
    mq_ref[...] = _proj_w(h, wmq_ref).astype(BF16)
    sz_ref[...] = _silu(_proj_w(h, wmz_ref)).astype(BF16)


def _proj_qkm(h, wt, q_off, k_off, al_off, mq_off, mz_off, tm):
    t, d = h.shape
    row = lambda i: (i, 0)
    out = lambda n: pl.BlockSpec((tm, n), row)
    fixed_rows = lambda rows, off: pl.BlockSpec(
        (pl.Element(rows), pl.Element(d)), lambda i: (off, 0), pipeline_mode=pl.Buffered(1))
    assert all(off % W_ROW_ALIGN == 0 for off in (q_off, k_off, al_off, mq_off, mz_off))
    return pl.pallas_call(
        _proj_qkm_kernel,
        grid=(t // tm,),
        in_specs=[pl.BlockSpec((tm, d), row),
                  fixed_rows(GLA_DK, q_off), fixed_rows(GLA_DK, k_off),
                  fixed_rows(MEM_WIDTH, mq_off), fixed_rows(MEM_WIDTH, mz_off),
                  fixed_rows(RANK_PAD, al_off)],
        out_specs=[out(GLA_DK), out(GLA_DK), out(MEM_WIDTH), out(MEM_WIDTH), out(RANK_PAD)],
        out_shape=[jax.ShapeDtypeStruct((t, GLA_DK), BF16), jax.ShapeDtypeStruct((t, GLA_DK), BF16),
                   jax.ShapeDtypeStruct((t, MEM_WIDTH), BF16),
                   jax.ShapeDtypeStruct((t, MEM_WIDTH), BF16),
                   jax.ShapeDtypeStruct((t, RANK_PAD), F32)],
        compiler_params=_params(("parallel",)),
        name="proj_qkm",
    )(h, wt, wt, wt, wt, wt)


def _dwconv_kernel(u_ref, w_ref, b_ref, o_ref, ext_ref, acc_ref, *, ts):
    s = pl.program_id(1)
    sub = ext_ref.shape[1]

    @pl.when(s == 0)
    def _():
        ext_ref[0:CONV_HALO] = jnp.zeros((CONV_HALO,) + ext_ref.shape[1:], F32)

    @pl.when(s > 0)
    def _():
        ext_ref[0:CONV_HALO] = ext_ref[ts:ts + CONV_HALO]

    ext_ref[CONV_HALO:CONV_HALO + ts] = pltpu.einshape(
        "t(sl)->tsl", u_ref[...].astype(F32), s=sub)
    lead = CONV_HALO - (CONV_K - 1)

    def body(i, carry):
        t0 = pl.multiple_of(i * CONV_TB, CONV_TB)
        acc = jnp.broadcast_to(b_ref[...], (CONV_TB,) + ext_ref.shape[1:])
        for j in range(CONV_K):
            acc = acc + w_ref[j] * ext_ref[pl.ds(t0 + lead + j, CONV_TB)]
        acc_ref[pl.ds(t0, CONV_TB)] = acc
        return carry

    lax.fori_loop(0, ts // CONV_TB, body, 0)
    o_ref[...] = pltpu.einshape("tsl->t(sl)", acc_ref[...]).astype(o_ref.dtype)


def _dwconv(u, w, b, batch, seq, ts):
    t, c = u.shape
    sub = c // LANES
    n_s = seq // ts
    tok = lambda bi, si: (bi * n_s + si, 0)
    return pl.pallas_call(
        functools.partial(_dwconv_kernel, ts=ts),
        grid=(batch, n_s),
        in_specs=[pl.BlockSpec((ts, c), tok), _fixed((CONV_K, sub, LANES)), _fixed((1, sub, LANES))],
        out_specs=pl.BlockSpec((ts, c), tok),
        out_shape=jax.ShapeDtypeStruct((t, c), BF16),
        scratch_shapes=[pltpu.VMEM((ts + CONV_HALO, sub, LANES), F32),
                        pltpu.VMEM((ts, sub, LANES), F32)],
        compiler_params=_params(("arbitrary", "arbitrary")),
        name="dwconv",
    )(u, w.reshape(CONV_K, sub, LANES), b.reshape(1, sub, LANES))


def _gla_constants():
    c = GLA_CHUNK
    t = np.arange(c)[:, None]
    u = np.arange(c)[None, :]
    mats = np.zeros((GLA_LEVELS + 1, c, c), np.float32)
    mats[0] = u <= t
    for lvl in range(GLA_LEVELS):
        m = 1 << lvl
        pos = t % (2 * m)
        r = t - pos + m - 1
        mats[1 + lvl] = ((pos >= m) & (u > r) & (u <= t)) | ((pos < m) & (u > t) & (u <= r))
    x = t ^ u
    lv = np.where(x > 0, np.floor(np.log2(np.maximum(x, 1))).astype(np.int32), GLA_LEVELS)
    lv = np.where(u > t, -1, lv).astype(np.int32)
    return mats.reshape((GLA_LEVELS + 1) * c, c), lv


def _gla_kernel(q_ref, k_ref, v_ref, la_ref, gz_ref, ng_ref, mats_ref, lv_ref, o_ref, st_ref):
    c = GLA_CHUNK

    @pl.when(pl.program_id(1) == 0)
    def _():
        st_ref[...] = jnp.zeros_like(st_ref)

    lv = lv_ref[...]
    for hd in range(GLA_HEADS):
        ks = slice(hd * GLA_DKH, (hd + 1) * GLA_DKH)
        vs = slice(hd * GLA_DVH, (hd + 1) * GLA_DVH)
        q = q_ref[:, ks]
        k = k_ref[:, ks]
        v = v_ref[:, vs]

        e_all = _dot(mats_ref[...], la_ref[:, ks])
        b2 = e_all[:c]
        b2_last = b2[c - 1:c, :]
        q_in = q * jnp.exp2(b2).astype(BF16)
        k_up = k * jnp.exp2(b2_last - b2).astype(BF16)

        st = st_ref[hd]
        o = _dot_nt(q_in, st.astype(BF16))

        p = jnp.where(lv == GLA_LEVELS, _dot_nt(q, k).astype(BF16), jnp.zeros((), BF16))
        for lvl in range(GLA_LEVELS):
            w = jnp.exp2(e_all[(1 + lvl) * c:(2 + lvl) * c]).astype(BF16)
            p = jnp.where(lv == lvl, _dot_nt(q * w, k * w).astype(BF16), p)
        o = o + _dot(p, v)

        st_ref[hd] = st * jnp.exp2(b2_last) + _dot_tn(v, k_up)

        y = o * _rms_scale(o) * ng_ref[:, vs]
        o_ref[:, vs] = (y * gz_ref[:, vs].astype(F32)).astype(o_ref.dtype)


def _gla(q, k, v, la2, gz, norm_g, batch, seq):
    t = q.shape[0]
    c = GLA_CHUNK
    n_c = seq // c
    mats, lv = _gla_constants()
    tok = lambda bi, ni: (bi * n_c + ni, 0)
    return pl.pallas_call(
        _gla_kernel,
        grid=(batch, n_c),
        in_specs=[pl.BlockSpec((c, GLA_DK), tok), pl.BlockSpec((c, GLA_DK), tok),
                  pl.BlockSpec((c, GLA_DV), tok), pl.BlockSpec((c, GLA_DK), tok),
                  pl.BlockSpec((c, GLA_DV), tok), _fixed((1, GLA_DV)),
                  _fixed(mats.shape), _fixed(lv.shape)],
        out_specs=pl.BlockSpec((c, GLA_DV), tok),
        out_shape=jax.ShapeDtypeStruct((t, GLA_DV), BF16),
        scratch_shapes=[pltpu.VMEM((GLA_HEADS, GLA_DVH, GLA_DKH), F32)],
        compiler_params=_params(("arbitrary", "arbitrary")),
        name="gla",
    )(q, k, v, la2, gz, norm_g.reshape(1, GLA_DV), jnp.asarray(mats, BF16), jnp.asarray(lv, BF16))


def _mem_kv_kernel(m_ref, g_ref, w_ref, o_ref):
    m = m_ref[...]
    mn = (m * _rms_scale(m) * g_ref[...]).astype(BF16)
    o_ref[...] = _dot(mn, w_ref[...]).astype(o_ref.dtype)


def _mem_kv(mem2, g, w, batch, n_mem):
    d = mem2.shape[1]
    n = w.shape[1]
    return pl.pallas_call(
        _mem_kv_kernel,
        grid=(batch,),
        in_specs=[pl.BlockSpec((n_mem, d), lambda bi: (bi, 0)), _fixed((1, d)), _fixed((d, n))],
        out_specs=pl.BlockSpec((n_mem, n), lambda bi: (bi, 0)),
        out_shape=jax.ShapeDtypeStruct((batch * n_mem, n), BF16),
        compiler_params=_params(("parallel",)),
        name="mem_kv",
    )(mem2, g.reshape(1, d), w)


def _branches_kernel(c_ref, csz_ref, og_ref, mq_ref, msz_ref, kv_ref, g_ref,
                     lg_ref, lb_ref, wc_ref, bc_ref, wg_ref, wm_ref, y_ref):
    d = D_MODEL
    c = c_ref[...].astype(F32)
    dev = c - jnp.mean(c, axis=-1, keepdims=True)
    var = jnp.mean(dev * dev, axis=-1, keepdims=True)
    yh = dev * lax.rsqrt(var + EPS) * lg_ref[...] + lb_ref[...]
    u = _silu(yh) * csz_ref[...].astype(F32)
    y = (_dot(u.astype(BF16), wc_ref[...]) + bc_ref[...]) * g_ref[:, 0:d].astype(F32)

    y = y + _dot(og_ref[...], wg_ref[...]) * g_ref[:, d:2 * d].astype(F32)

    q = mq_ref[...]
    kv = kv_ref[...]
    outs = []
    for hd in range(MEM_HEADS):
        lo, hi = hd * MEM_HD, (hd + 1) * MEM_HD
        sc = _dot_nt(q[:, lo:hi], kv[:, lo:hi]) * (MEM_HD ** -0.5)
        e = jnp.exp(sc - jnp.max(sc, axis=-1, keepdims=True))
        p = e / jnp.sum(e, axis=-1, keepdims=True)
        outs.append(_dot(p.astype(BF16), kv[:, MEM_WIDTH + lo:MEM_WIDTH + hi]))
    om = jnp.concatenate(outs, axis=-1) * msz_ref[...].astype(F32)
    y = y + _dot(om.astype(BF16), wm_ref[...]) * g_ref[:, 2 * d:3 * d].astype(F32)
    y_ref[...] = y.astype(y_ref.dtype)


def _branches(conv, conv_sz, og, mq, mem_sz, mkv, gates, ln_g, ln_b, w_conv, b_conv, w_gla,
              w_mem, batch, seq, n_mem, tm):
    t, d = conv.shape
    n_t = seq // tm
    tok = lambda bi, ti: (bi * n_t + ti, 0)
    wide = lambda w: pl.BlockSpec((tm, w), tok)
    return pl.pallas_call(
        _branches_kernel,
        grid=(batch, n_t),
        in_specs=[wide(d), wide(d), wide(GLA_DV), wide(MEM_WIDTH), wide(MEM_WIDTH),
                  pl.BlockSpec((n_mem, 2 * MEM_WIDTH), lambda bi, ti: (bi, 0)),
                  wide(N_BRANCH * d),
                  _fixed((1, d)), _fixed((1, d)), _fixed((d, d)), _fixed((1, d)),
                  _fixed((GLA_DV, d)), _fixed((MEM_WIDTH, d))],
        out_specs=wide(d),
        out_shape=jax.ShapeDtypeStruct((t, d), BF16),
        compiler_params=_params(("parallel", "parallel")),
        name="branches",
    )(conv, conv_sz, og, mq, mem_sz, mkv, gates, ln_g.reshape(1, d), ln_b.reshape(1, d),
      w_conv, b_conv.reshape(1, d), w_gla, w_mem)


def _out_kernel(y_ref, x_ref, w_ref, g_ref, o_ref):
    r = x_ref[...] + _dot(y_ref[...], w_ref[...])
    o_ref[...] = r * _rms_scale(r) * g_ref[...]


def _out_proj(y, x, w, g, tm):
    t, d = x.shape
    row = lambda i: (i, 0)
    return pl.pallas_call(
        _out_kernel,
        grid=(t // tm,),
        in_specs=[pl.BlockSpec((tm, d), row), pl.BlockSpec((tm, d), row), _fixed((d, d)),
                  _fixed((1, d))],
        out_specs=pl.BlockSpec((tm, d), row),
        out_shape=jax.ShapeDtypeStruct((t, d), F32),
        compiler_params=_params(("parallel",)),
        name="out_proj",
    )(y, x, w, g.reshape(1, d))


def kernel(x, mem, ln_in_g, mem_ln_g, w_in, b_gate, dw_w, dw_b, conv_ln_g, conv_ln_b,
           w_conv_out, b_conv_out, w_alpha2, b_alpha, gla_norm_g, w_gla_out,
           w_mem_kv, w_mem_out, w_out, final_g):
    batch, seq, d = x.shape
    n_mem = mem.shape[1]
    depth = w_in.shape[0]
    assert depth == 1 and d == D_MODEL
    t = batch * seq
    xs = x.reshape(t, d)
    mem2 = mem.reshape(batch * n_mem, d)
    bf = lambda a: a.astype(BF16)
    l = 0

    widths = [d, d, d, GLA_DK, GLA_DK, GLA_DV, GLA_DV, GLA_RANK, MEM_WIDTH, MEM_WIDTH, N_BRANCH * d]
    offs = [int(o) for o in np.cumsum([0] + widths)]
    (o_ca, o_cb, o_cz, o_q, o_k, o_v, o_gz, o_al, o_mq, o_mz, o_g, o_end) = offs
    wt = w_in[l].T
    w2_pad = jnp.zeros((RANK_PAD, GLA_DK), F32).at[:GLA_RANK].set(w_alpha2[l])

    h = _rmsnorm_bf16(xs, ln_in_g[l], tm=512)

    u, conv_sz = _proj_call(_proj_conv_kernel, "proj_conv", h, wt, [o_ca, o_cb, o_cz], d,
                            [BF16, BF16], tm=1024, tn=512)
    q, k, mq, mem_sz, alpha = _proj_qkm(h, wt, o_q, o_k, o_al, o_mq, o_mz, tm=512)
    la2 = _log_decay(alpha, w2_pad, b_alpha[l], tm=1024)
    v, gla_sz = _proj_call(_proj_vz_kernel, "proj_vz", h, wt, [o_v, o_gz], GLA_DV,
                           [BF16, BF16], tm=1024, tn=512)
    (gates,) = _proj_call(_proj_gate_kernel, "proj_gate", h, wt, [o_g], N_BRANCH * d, [BF16],
                          tm=1024, tn=1024, bias=b_gate[l])

    conv = _dwconv(u, dw_w[l], dw_b[l], batch, seq, ts=512)
    og = _gla(q, k, v, la2, gla_sz, gla_norm_g[l], batch, seq)
    mkv = _mem_kv(mem2, mem_ln_g[l], bf(w_mem_kv[l]), batch, n_mem)

    y = _branches(conv, conv_sz, og, mq, mem_sz, mkv, gates, conv_ln_g[l], conv_ln_b[l],
                  bf(w_conv_out[l]), b_conv_out[l], bf(w_gla_out[l]), bf(w_mem_out[l]),
                  batch, seq, n_mem, tm=512)
    out = _out_proj(y, xs, bf(w_out[l]), final_g, tm=512)
    return out.reshape(batch, seq, d)
```

```python
import functools

import numpy as np
import jax
import jax.numpy as jnp
from jax import lax
from jax.experimental import pallas as pl
from jax.experimental.pallas import tpu as pltpu

F32 = jnp.float32
BF16 = jnp.bfloat16

D_MODEL = 2048
EPS = 1e-6
CONV_K = 31
GLA_HEADS = 4
GLA_DK = 1024
GLA_DV = 2048
GLA_DKH = GLA_DK // GLA_HEADS
GLA_DVH = GLA_DV // GLA_HEADS
GLA_RANK = 16
GLA_TAU = 16.0
MEM_HEADS = 4
MEM_HD = 128
MEM_WIDTH = MEM_HEADS * MEM_HD
N_BRANCH = 3

LANES = 128
VMEM_LIMIT = 60 * 1024 * 1024

GLA_CHUNK = 256
GLA_LEVELS = 8
RANK_PAD = LANES
CONV_HALO = 32
CONV_TB = 8
LOG2_E = 1.4426950408889634
W_ROW_ALIGN = 16


def _params(sem):
    return pltpu.CompilerParams(dimension_semantics=sem, vmem_limit_bytes=VMEM_LIMIT)


def _fixed(shape, index=None):
    index = (0,) * len(shape) if index is None else tuple(index)
    return pl.BlockSpec(shape, lambda *_: index, pipeline_mode=pl.Buffered(1))


def _sigmoid(x):
    return 1.0 / (1.0 + jnp.exp(-x))


def _silu(x):
    return x * _sigmoid(x)


def _dot(a, b):
    return jnp.dot(a, b, preferred_element_type=F32)


def _dot_nt(a, b):
    return lax.dot_general(a, b, (((1,), (1,)), ((), ())), preferred_element_type=F32)


def _dot_tn(a, b):
    return lax.dot_general(a, b, (((0,), (0,)), ((), ())), preferred_element_type=F32)


def _split_bf16(x):
    hi = x.astype(BF16)
    return hi, (x - hi.astype(F32)).astype(BF16)


def _rms_scale(x):
    return lax.rsqrt(jnp.mean(x * x, axis=-1, keepdims=True) + EPS)


def _rmsnorm_kernel(x_ref, g_ref, o_ref):
    x = x_ref[...]
    o_ref[...] = (x * _rms_scale(x) * g_ref[...]).astype(o_ref.dtype)


def _rmsnorm_bf16(x, g, tm):
    t, d = x.shape
    return pl.pallas_call(
        _rmsnorm_kernel,
        grid=(t // tm,),
        in_specs=[pl.BlockSpec((tm, d), lambda i: (i, 0)), _fixed((1, d))],
        out_specs=pl.BlockSpec((tm, d), lambda i: (i, 0)),
        out_shape=jax.ShapeDtypeStruct((t, d), BF16),
        compiler_params=_params(("parallel",)),
        name="rmsnorm_in",
    )(x, g.reshape(1, d))


def _proj_w(h, wt_ref):
    return _dot_nt(h, wt_ref[...].astype(BF16))


def _round_weights_once(wt_refs, wb_ref):
    @pl.when(pl.program_id(1) == 0)
    def _():
        for n, wt_ref in enumerate(wt_refs):
            wb_ref[n] = wt_ref[...].T.astype(BF16)


def _proj_conv_kernel(h_ref, wa_ref, wb_ref, wz_ref, u_ref, sz_ref, w_ref):
    _round_weights_once((wa_ref, wb_ref, wz_ref), w_ref)
    h = h_ref[...]
    u_ref[...] = (_dot(h, w_ref[0]) * _sigmoid(_dot(h, w_ref[1]))).astype(BF16)
    sz_ref[...] = _silu(_dot(h, w_ref[2])).astype(BF16)


def _proj_vz_kernel(h_ref, wv_ref, wz_ref, v_ref, sz_ref, w_ref):
    _round_weights_once((wv_ref, wz_ref), w_ref)
    h = h_ref[...]
    v_ref[...] = _dot(h, w_ref[0]).astype(BF16)
    sz_ref[...] = _silu(_dot(h, w_ref[1])).astype(BF16)


def _proj_gate_kernel(h_ref, wg_ref, b_ref, g_ref, w_ref):
    _round_weights_once((wg_ref,), w_ref)
    g_ref[...] = _sigmoid(_dot(h_ref[...], w_ref[0]) + b_ref[...]).astype(BF16)


def _wt_rows(rows, d, index_map):
    def im(*idx):
        return pl.multiple_of(index_map(*idx), W_ROW_ALIGN), 0
    return pl.BlockSpec((pl.Element(rows), pl.Element(d)), im)


def _proj_call(kernel, name, h, wt, row_offsets, width, out_dtypes, tm, tn, bias=None):
    t, d = h.shape
    assert all(off % W_ROW_ALIGN == 0 for off in row_offsets) and width % tn == 0
    in_specs = [pl.BlockSpec((tm, d), lambda j, i: (i, 0))]
    in_specs += [_wt_rows(tn, d, lambda j, i, o=off: o + j * tn) for off in row_offsets]
    args = [h] + [wt] * len(row_offsets)
    if bias is not None:
        in_specs.append(pl.BlockSpec((1, tn), lambda j, i: (0, j)))
        args.append(bias.reshape(1, width))
    return pl.pallas_call(
        kernel,
        grid=(width // tn, t // tm),
        in_specs=in_specs,
        out_specs=[pl.BlockSpec((tm, tn), lambda j, i: (i, j)) for _ in out_dtypes],
        out_shape=[jax.ShapeDtypeStruct((t, width), dt) for dt in out_dtypes],
        scratch_shapes=[pltpu.VMEM((len(row_offsets), d, tn), BF16)],
        compiler_params=_params(("arbitrary", "arbitrary")),
        name=name,
    )(*args)


def _log_decay_kernel(al_ref, w2_ref, ba_ref, la_ref):
    al_hi, al_lo = _split_bf16(al_ref[...])
    w2_hi, w2_lo = _split_bf16(w2_ref[...])
    x = _dot(al_hi, w2_hi) + _dot(al_lo, w2_hi) + _dot(al_hi, w2_lo) + ba_ref[...]
    log_a = (jnp.minimum(x, 0.0) - jnp.log1p(jnp.exp(-jnp.abs(x)))) * (1.0 / GLA_TAU)
    la_ref[...] = (log_a * LOG2_E).astype(BF16)


def _log_decay(alpha, w2_pad, b_alpha, tm):
    t = alpha.shape[0]
    return pl.pallas_call(
        _log_decay_kernel,
        grid=(t // tm,),
        in_specs=[pl.BlockSpec((tm, RANK_PAD), lambda i: (i, 0)), _fixed((RANK_PAD, GLA_DK)),
                  _fixed((1, GLA_DK))],
        out_specs=pl.BlockSpec((tm, GLA_DK), lambda i: (i, 0)),
        out_shape=jax.ShapeDtypeStruct((t, GLA_DK), BF16),
        compiler_params=_params(("parallel",)),
        name="log_decay",
    )(alpha, w2_pad, b_alpha.reshape(1, GLA_DK))


def _proj_qkm_kernel(h_ref, wq_ref, wk_ref, wmq_ref, wmz_ref, wal_ref,
                     q_ref, k_ref, mq_ref, sz_ref, al_ref):
    h = h_ref[...]
    al_ref[...] = _proj_w(h, wal_ref)
    q_ref[...] = (_proj_w(h, wq_ref) * (GLA_DKH ** -0.5)).astype(BF16)
    k_ref[...] = _proj_w(h, wk_ref).astype(BF16)
    mq_ref[...] = _proj_w(h, wmq_ref).astype(BF16)
    sz_ref[...] = _silu(_proj_w(h, wmz_ref)).astype(BF16)


def _proj_qkm(h, wt, q_off, k_off, al_off, mq_off, mz_off, tm):
    t, d = h.shape
    row = lambda i: (i, 0)
    out = lambda n: pl.BlockSpec((tm, n), row)
    fixed_rows = lambda rows, off: pl.BlockSpec(
        (pl.Element(rows), pl.Element(d)), lambda i: (off, 0), pipeline_mode=pl.Buffered(1))
    assert all(off % W_ROW_ALIGN == 0 for off in (q_off, k_off, al_off, mq_off, mz_off))
    return pl.pallas_call(
        _proj_qkm_kernel,
        grid=(t // tm,),
        in_specs=[pl.BlockSpec((tm, d), row),
                  fixed_rows(GLA_DK, q_off), fixed_rows(GLA_DK, k_off),
                  fixed_rows(MEM_WIDTH, mq_off), fixed_rows(MEM_WIDTH, mz_off),
                  fixed_rows(RANK_PAD, al_off)],
        out_specs=[out(GLA_DK), out(GLA_DK), out(MEM_WIDTH), out(MEM_WIDTH), out(RANK_PAD)],
        out_shape=[jax.ShapeDtypeStruct((t, GLA_DK), BF16), jax.ShapeDtypeStruct((t, GLA_DK), BF16),
                   jax.ShapeDtypeStruct((t, MEM_WIDTH), BF16),
                   jax.ShapeDtypeStruct((t, MEM_WIDTH), BF16),
                   jax.ShapeDtypeStruct((t, RANK_PAD), F32)],
        compiler_params=_params(("parallel",)),
        name="proj_qkm",
    )(h, wt, wt, wt, wt, wt)


def _dwconv_kernel(u_ref, w_ref, b_ref, o_ref, ext_ref, acc_ref, *, ts):
    s = pl.program_id(1)
    sub = ext_ref.shape[1]

    @pl.when(s == 0)
    def _():
        ext_ref[0:CONV_HALO] = jnp.zeros((CONV_HALO,) + ext_ref.shape[1:], F32)

    @pl.when(s > 0)
    def _():
        ext_ref[0:CONV_HALO] = ext_ref[ts:ts + CONV_HALO]

    ext_ref[CONV_HALO:CONV_HALO + ts] = pltpu.einshape(
        "t(sl)->tsl", u_ref[...].astype(F32), s=sub)
    lead = CONV_HALO - (CONV_K - 1)

    def body(i, carry):
        t0 = pl.multiple_of(i * CONV_TB, CONV_TB)
        acc = jnp.broadcast_to(b_ref[...], (CONV_TB,) + ext_ref.shape[1:])
        for j in range(CONV_K):
            acc = acc + w_ref[j] * ext_ref[pl.ds(t0 + lead + j, CONV_TB)]
        acc_ref[pl.ds(t0, CONV_TB)] = acc
        return carry

    lax.fori_loop(0, ts // CONV_TB, body, 0)
    o_ref[...] = pltpu.einshape("tsl->t(sl)", acc_ref[...]).astype(o_ref.dtype)


def _dwconv(u, w, b, batch, seq, ts):
    t, c = u.shape
    sub = c // LANES
    n_s = seq // ts
    tok = lambda bi, si: (bi * n_s + si, 0)
    return pl.pallas_call(
        functools.partial(_dwconv_kernel, ts=ts),
        grid=(batch, n_s),
        in_specs=[pl.BlockSpec((ts, c), tok), _fixed((CONV_K, sub, LANES)), _fixed((1, sub, LANES))],
        out_specs=pl.BlockSpec((ts, c), tok),
        out_shape=jax.ShapeDtypeStruct((t, c), BF16),
        scratch_shapes=[pltpu.VMEM((ts + CONV_HALO, sub, LANES), F32),
                        pltpu.VMEM((ts, sub, LANES), F32)],
        compiler_params=_params(("arbitrary", "arbitrary")),
        name="dwconv",
    )(u, w.reshape(CONV_K, sub, LANES), b.reshape(1, sub, LANES))


def _gla_constants():
    c = GLA_CHUNK
    t = np.arange(c)[:, None]
    u = np.arange(c)[None, :]
    mats = np.zeros((GLA_LEVELS + 1, c, c), np.float32)
    mats[0] = u <= t
    for lvl in range(GLA_LEVELS):
        m = 1 << lvl
        pos = t % (2 * m)
        r = t - pos + m - 1
        mats[1 + lvl] = ((pos >= m) & (u > r) & (u <= t)) | ((pos < m) & (u > t) & (u <= r))
    x = t ^ u
    lv = np.where(x > 0, np.floor(np.log2(np.maximum(x, 1))).astype(np.int32), GLA_LEVELS)
    lv = np.where(u > t, -1, lv).astype(np.int32)
    return mats.reshape((GLA_LEVELS + 1) * c, c), lv


def _gla_kernel(q_ref, k_ref, v_ref, la_ref, gz_ref, ng_ref, mats_ref, lv_ref, o_ref, st_ref):
    c = GLA_CHUNK

    @pl.when(pl.program_id(1) == 0)
    def _():
        st_ref[...] = jnp.zeros_like(st_ref)

    lv = lv_ref[...]
    for hd in range(GLA_HEADS):
        ks = slice(hd * GLA_DKH, (hd + 1) * GLA_DKH)
        vs = slice(hd * GLA_DVH, (hd + 1) * GLA_DVH)
        q = q_ref[:, ks]
        k = k_ref[:, ks]
        v = v_ref[:, vs]

        e_all = _dot(mats_ref[...], la_ref[:, ks])
        b2 = e_all[:c]
        b2_last = b2[c - 1:c, :]
        q_in = q * jnp.exp2(b2).astype(BF16)
        k_up = k * jnp.exp2(b2_last - b2).astype(BF16)

        st = st_ref[hd]
        o = _dot_nt(q_in, st.astype(BF16))

        p = jnp.where(lv == GLA_LEVELS, _dot_nt(q, k).astype(BF16), jnp.zeros((), BF16))
        for lvl in range(GLA_LEVELS):
            w = jnp.exp2(e_all[(1 + lvl) * c:(2 + lvl) * c]).astype(BF16)
            p = jnp.where(lv == lvl, _dot_nt(q * w, k * w).astype(BF16), p)
        o = o + _dot(p, v)

        st_ref[hd] = st * jnp.exp2(b2_last) + _dot_tn(v, k_up)

        y = o * _rms_scale(o) * ng_ref[:, vs]
        o_ref[:, vs] = (y * gz_ref[:, vs].astype(F32)).astype(o_ref.dtype)


def _gla(q, k, v, la2, gz, norm_g, batch, seq):
    t = q.shape[0]
    c = GLA_CHUNK
    n_c = seq // c
    mats, lv = _gla_constants()
    tok = lambda bi, ni: (bi * n_c + ni, 0)
    return pl.pallas_call(
        _gla_kernel,
        grid=(batch, n_c),
        in_specs=[pl.BlockSpec((c, GLA_DK), tok), pl.BlockSpec((c, GLA_DK), tok),
                  pl.BlockSpec((c, GLA_DV), tok), pl.BlockSpec((c, GLA_DK), tok),
                  pl.BlockSpec((c, GLA_DV), tok), _fixed((1, GLA_DV)),
                  _fixed(mats.shape), _fixed(lv.shape)],
        out_specs=pl.BlockSpec((c, GLA_DV), tok),
        out_shape=jax.ShapeDtypeStruct((t, GLA_DV), BF16),
        scratch_shapes=[pltpu.VMEM((GLA_HEADS, GLA_DVH, GLA_DKH), F32)],
        compiler_params=_params(("arbitrary", "arbitrary")),
        name="gla",
    )(q, k, v, la2, gz, norm_g.reshape(1, GLA_DV), jnp.asarray(mats, BF16), jnp.asarray(lv, BF16))


def _mem_kv_kernel(m_ref, g_ref, w_ref, o_ref):
    m = m_ref[...]
    mn = (m * _rms_scale(m) * g_ref[...]).astype(BF16)
    o_ref[...] = _dot(mn, w_ref[...]).astype(o_ref.dtype)


def _mem_kv(mem2, g, w, batch, n_mem):
    d = mem2.shape[1]
    n = w.shape[1]
    return pl.pallas_call(
        _mem_kv_kernel,
        grid=(batch,),
        in_specs=[pl.BlockSpec((n_mem, d), lambda bi: (bi, 0)), _fixed((1, d)), _fixed((d, n))],
        out_specs=pl.BlockSpec((n_mem, n), lambda bi: (bi, 0)),
        out_shape=jax.ShapeDtypeStruct((batch * n_mem, n), BF16),
        compiler_params=_params(("parallel",)),
        name="mem_kv",
    )(mem2, g.reshape(1, d), w)


def _branches_kernel(c_ref, csz_ref, og_ref, mq_ref, msz_ref, kv_ref, g_ref,
                     lg_ref, lb_ref, wc_ref, bc_ref, wg_ref, wm_ref, y_ref):
    d = D_MODEL
    c = c_ref[...].astype(F32)
    dev = c - jnp.mean(c, axis=-1, keepdims=True)
    var = jnp.mean(dev * dev, axis=-1, keepdims=True)
    yh = dev * lax.rsqrt(var + EPS) * lg_ref[...] + lb_ref[...]
    u = _silu(yh) * csz_ref[...].astype(F32)
    y = (_dot(u.astype(BF16), wc_ref[...]) + bc_ref[...]) * g_ref[:, 0:d].astype(F32)

    y = y + _dot(og_ref[...], wg_ref[...]) * g_ref[:, d:2 * d].astype(F32)

    q = mq_ref[...]
    kv = kv_ref[...]
    outs = []
    for hd in range(MEM_HEADS):
        lo, hi = hd * MEM_HD, (hd + 1) * MEM_HD
        sc = _dot_nt(q[:, lo:hi], kv[:, lo:hi]) * (MEM_HD ** -0.5)
        e = jnp.exp(sc - jnp.max(sc, axis=-1, keepdims=True))
        p = e / jnp.sum(e, axis=-1, keepdims=True)
        outs.append(_dot(p.astype(BF16), kv[:, MEM_WIDTH + lo:MEM_WIDTH + hi]))
    om = jnp.concatenate(outs, axis=-1) * msz_ref[...].astype(F32)
    y = y + _dot(om.astype(BF16), wm_ref[...]) * g_ref[:, 2 * d:3 * d].astype(F32)
    y_ref[...] = y.astype(y_ref.dtype)


def _branches(conv, conv_sz, og, mq, mem_sz, mkv, gates, ln_g, ln_b, w_conv, b_conv, w_gla,
              w_mem, batch, seq, n_mem, tm):
    t, d = conv.shape
    n_t = seq // tm
    tok = lambda bi, ti: (bi * n_t + ti, 0)
    wide = lambda w: pl.BlockSpec((tm, w), tok)
    return pl.pallas_call(
        _branches_kernel,
        grid=(batch, n_t),
        in_specs=[wide(d), wide(d), wide(GLA_DV), wide(MEM_WIDTH), wide(MEM_WIDTH),
                  pl.BlockSpec((n_mem, 2 * MEM_WIDTH), lambda bi, ti: (bi, 0)),
                  wide(N_BRANCH * d),
                  _fixed((1, d)), _fixed((1, d)), _fixed((d, d)), _fixed((1, d)),
                  _fixed((GLA_DV, d)), _fixed((MEM_WIDTH, d))],
        out_specs=wide(d),
        out_shape=jax.ShapeDtypeStruct((t, d), BF16),
        compiler_params=_params(("parallel", "parallel")),
        name="branches",
    )(conv, conv_sz, og, mq, mem_sz, mkv, gates, ln_g.reshape(1, d), ln_b.reshape(1, d),
      w_conv, b_conv.reshape(1, d), w_gla, w_mem)


def _out_kernel(y_ref, x_ref, w_ref, g_ref, o_ref):
    r = x_ref[...] + _dot(y_ref[...], w_ref[...])
    o_ref[...] = r * _rms_scale(r) * g_ref[...]


def _out_proj(y, x, w, g, tm):
    t, d = x.shape
    row = lambda i: (i, 0)
    return pl.pallas_call(
        _out_kernel,
        grid=(t // tm,),
        in_specs=[pl.BlockSpec((tm, d), row), pl.BlockSpec((tm, d), row), _fixed((d, d)),
                  _fixed((1, d))],
        out_specs=pl.BlockSpec((tm, d), row),
        out_shape=jax.ShapeDtypeStruct((t, d), F32),
        compiler_params=_params(("parallel",)),
        name="out_proj",
    )(y, x, w, g.reshape(1, d))


def kernel(x, mem, ln_in_g, mem_ln_g, w_in, b_gate, dw_w, dw_b, conv_ln_g, conv_ln_b,
           w_conv_out, b_conv_out, w_alpha2, b_alpha, gla_norm_g, w_gla_out,
           w_mem_kv, w_mem_out, w_out, final_g):
    batch, seq, d = x.shape
    n_mem = mem.shape[1]
    depth = w_in.shape[0]
    assert depth == 1 and d == D_MODEL
    t = batch * seq
    xs = x.reshape(t, d)
    mem2 = mem.reshape(batch * n_mem, d)
    bf = lambda a: a.astype(BF16)
    l = 0

    widths = [d, d, d, GLA_DK, GLA_DK, GLA_DV, GLA_DV, GLA_RANK, MEM_WIDTH, MEM_WIDTH, N_BRANCH * d]
    offs = [int(o) for o in np.cumsum([0] + widths)]
    (o_ca, o_cb, o_cz, o_q, o_k, o_v, o_gz, o_al, o_mq, o_mz, o_g, o_end) = offs
    wt = w_in[l].T
    w2_pad = jnp.zeros((RANK_PAD, GLA_DK), F32).at[:GLA_RANK].set(w_alpha2[l])

    h = _rmsnorm_bf16(xs, ln_in_g[l], tm=512)

    u, conv_sz = _proj_call(_proj_conv_kernel, "proj_conv", h, wt, [o_ca, o_cb, o_cz], d,
                            [BF16, BF16], tm=1024, tn=512)
    q, k, mq, mem_sz, alpha = _proj_qkm(h, wt, o_q, o_k, o_al, o_mq, o_mz, tm=512)
    la2 = _log_decay(alpha, w2_pad, b_alpha[l], tm=1024)
    v, gla_sz = _proj_call(_proj_vz_kernel, "proj_vz", h, wt, [o_v, o_gz], GLA_DV,
                           [BF16, BF16], tm=1024, tn=512)
    (gates,) = _proj_call(_proj_gate_kernel, "proj_gate", h, wt, [o_g], N_BRANCH * d, [BF16],
                          tm=1024, tn=1024, bias=b_gate[l])

    conv = _dwconv(u, dw_w[l], dw_b[l], batch, seq, ts=512)
    og = _gla(q, k, v, la2, gla_sz, gla_norm_g[l], batch, seq)
    mkv = _mem_kv(mem2, mem_ln_g[l], bf(w_mem_kv[l]), batch, n_mem)

    y = _branches(conv, conv_sz, og, mq, mem_sz, mkv, gates, conv_ln_g[l], conv_ln_b[l],
                  bf(w_conv_out[l]), b_conv_out[l], bf(w_gla_out[l]), bf(w_mem_out[l]),
                  batch, seq, n_mem, tm=512)
    out = _out_proj(y, xs, bf(w_out[l]), final_g, tm=512)
    return out.reshape(batch, seq, d)
```

```python
import functools

import numpy as np
import jax
import jax.numpy as jnp
from jax import lax
from jax.experimental import pallas as pl
from jax.experimental.pallas import tpu as pltpu

F32 = jnp.float32
BF16 = jnp.bfloat16

D_MODEL = 2048
EPS = 1e-6
CONV_K = 31
GLA_HEADS = 4
GLA_DK = 1024
GLA_DV = 2048
GLA_DKH = GLA_DK // GLA_HEADS
GLA_DVH = GLA_DV // GLA_HEADS
GLA_RANK = 16
GLA_TAU = 16.0
MEM_HEADS = 4
MEM_HD = 128
MEM_WIDTH = MEM_HEADS * MEM_HD
N_BRANCH = 3

LANES = 128
VMEM_LIMIT = 60 * 1024 * 1024

GLA_CHUNK = 256
GLA_LEVELS = 8
RANK_PAD = LANES
CONV_HALO = 32
CONV_TB = 8
LOG2_E = 1.4426950408889634
W_ROW_ALIGN = 16


def _params(sem):
    return pltpu.CompilerParams(dimension_semantics=sem, vmem_limit_bytes=VMEM_LIMIT)


def _fixed(shape, index=None):
    index = (0,) * len(shape) if index is None else tuple(index)
    return pl.BlockSpec(shape, lambda *_: index, pipeline_mode=pl.Buffered(1))


def _sigmoid(x):
    return 1.0 / (1.0 + jnp.exp(-x))


def _silu(x):
    return x * _sigmoid(x)


def _dot(a, b):
    return jnp.dot(a, b, preferred_element_type=F32)


def _dot_nt(a, b):
    return lax.dot_general(a, b, (((1,), (1,)), ((), ())), preferred_element_type=F32)


def _dot_tn(a, b):
    return lax.dot_general(a, b, (((0,), (0,)), ((), ())), preferred_element_type=F32)


def _rms_scale(x):
    return lax.rsqrt(jnp.mean(x * x, axis=-1, keepdims=True) + EPS)


def _round_weights_once(wt_refs, wb_ref):
    @pl.when(pl.program_id(1) == 0)
    def _():
        for n, wt_ref in enumerate(wt_refs):
            wb_ref[n] = wt_ref[...].T.astype(BF16)


def _proj_conv_kernel(h_ref, wa_ref, wb_ref, wz_ref, u_ref, sz_ref, w_ref):
    _round_weights_once((wa_ref, wb_ref, wz_ref), w_ref)
    h = h_ref[...]
    u_ref[...] = (_dot(h, w_ref[0]) * _sigmoid(_dot(h, w_ref[1]))).astype(BF16)
    sz_ref[...] = _silu(_dot(h, w_ref[2])).astype(BF16)


def _proj_vz_kernel(h_ref, wv_ref, wz_ref, v_ref, sz_ref, w_ref):
    _round_weights_once((wv_ref, wz_ref), w_ref)
    h = h_ref[...]
    v_ref[...] = _dot(h, w_ref[0]).astype(BF16)
    sz_ref[...] = _silu(_dot(h, w_ref[1])).astype(BF16)


def _proj_gate_kernel(h_ref, wg_ref, b_ref, g_ref, w_ref):
    _round_weights_once((wg_ref,), w_ref)
    g_ref[...] = _sigmoid(_dot(h_ref[...], w_ref[0]) + b_ref[...]).astype(BF16)


def _wt_rows(rows, d, index_map):
    def im(*idx):
        return pl.multiple_of(index_map(*idx), W_ROW_ALIGN), 0
    return pl.BlockSpec((pl.Element(rows), pl.Element(d)), im)


def _proj_call(kernel, name, h, wt, row_offsets, width, out_dtypes, tm, tn, bias=None):
    t, d = h.shape
    assert all(off % W_ROW_ALIGN == 0 for off in row_offsets) and width % tn == 0
    in_specs = [pl.BlockSpec((tm, d), lambda j, i: (i, 0))]
    in_specs += [_wt_rows(tn, d, lambda j, i, o=off: o + j * tn) for off in row_offsets]
    args = [h] + [wt] * len(row_offsets)
    if bias is not None:
        in_specs.append(pl.BlockSpec((1, tn), lambda j, i: (0, j)))
        args.append(bias.reshape(1, width))
    return pl.pallas_call(
        kernel,
        grid=(width // tn, t // tm),
        in_specs=in_specs,
        out_specs=[pl.BlockSpec((tm, tn), lambda j, i: (i, j)) for _ in out_dtypes],
        out_shape=[jax.ShapeDtypeStruct((t, width), dt) for dt in out_dtypes],
        scratch_shapes=[pltpu.VMEM((len(row_offsets), d, tn), BF16)],
        compiler_params=_params(("arbitrary", "arbitrary")),
        name=name,
    )(*args)


def _norm_qkm_kernel(x_ref, g_ref, wqk_ref, wm_ref, h_ref, q_ref, k_ref, mq_ref, sz_ref, al_ref):
    x = x_ref[...]
    h = (x * _rms_scale(x) * g_ref[...]).astype(BF16)
    h_ref[...] = h
    al_ref[...] = _dot_nt(h, wm_ref[0:RANK_PAD])
    q_ref[...] = (_dot_nt(h, wqk_ref[0:GLA_DK]) * (GLA_DKH ** -0.5)).astype(BF16)
    k_ref[...] = _dot_nt(h, wqk_ref[GLA_DK:2 * GLA_DK]).astype(BF16)
    mq_ref[...] = _dot_nt(h, wm_ref[GLA_RANK:GLA_RANK + MEM_WIDTH]).astype(BF16)
    sz_ref[...] = _silu(_dot_nt(h, wm_ref[GLA_RANK + MEM_WIDTH:GLA_RANK + 2 * MEM_WIDTH])).astype(BF16)


def _norm_qkm(x, g, w_qk, w_m, tm):
    t, d = x.shape
    row = lambda i: (i, 0)
    out = lambda n: pl.BlockSpec((tm, n), row)
    return pl.pallas_call(
        _norm_qkm_kernel,
        grid=(t // tm,),
        in_specs=[pl.BlockSpec((tm, d), row), _fixed((1, d)), _fixed(w_qk.shape), _fixed(w_m.shape)],
        out_specs=[out(d), out(GLA_DK), out(GLA_DK), out(MEM_WIDTH), out(MEM_WIDTH), out(RANK_PAD)],
        out_shape=[jax.ShapeDtypeStruct((t, d), BF16),
                   jax.ShapeDtypeStruct((t, GLA_DK), BF16), jax.ShapeDtypeStruct((t, GLA_DK), BF16),
                   jax.ShapeDtypeStruct((t, MEM_WIDTH), BF16),
                   jax.ShapeDtypeStruct((t, MEM_WIDTH), BF16),
                   jax.ShapeDtypeStruct((t, RANK_PAD), F32)],
        compiler_params=_params(("parallel",)),
        name="norm_qkm",
    )(x, g.reshape(1, d), w_qk, w_m)


def _dwconv_kernel(u_ref, w_ref, b_ref, o_ref, ext_ref, acc_ref, *, ts):
    s = pl.program_id(1)
    sub = ext_ref.shape[1]

    @pl.when(s == 0)
    def _():
        ext_ref[0:CONV_HALO] = jnp.zeros((CONV_HALO,) + ext_ref.shape[1:], F32)

    @pl.when(s > 0)
    def _():
        ext_ref[0:CONV_HALO] = ext_ref[ts:ts + CONV_HALO]

    ext_ref[CONV_HALO:CONV_HALO + ts] = pltpu.einshape(
        "t(sl)->tsl", u_ref[...].astype(F32), s=sub)
    lead = CONV_HALO - (CONV_K - 1)

    def body(i, carry):
        t0 = pl.multiple_of(i * CONV_TB, CONV_TB)
        acc = jnp.broadcast_to(b_ref[...], (CONV_TB,) + ext_ref.shape[1:])
        for j in range(CONV_K):
            acc = acc + w_ref[j] * ext_ref[pl.ds(t0 + lead + j, CONV_TB)]
        acc_ref[pl.ds(t0, CONV_TB)] = acc
        return carry

    lax.fori_loop(0, ts // CONV_TB, body, 0)
    o_ref[...] = pltpu.einshape("tsl->t(sl)", acc_ref[...]).astype(o_ref.dtype)


def _dwconv(u, w, b, batch, seq, ts):
    t, c = u.shape
    sub = c // LANES
    n_s = seq // ts
    tok = lambda bi, si: (bi * n_s + si, 0)
    return pl.pallas_call(
        functools.partial(_dwconv_kernel, ts=ts),
        grid=(batch, n_s),
        in_specs=[pl.BlockSpec((ts, c), tok), _fixed((CONV_K, sub, LANES)), _fixed((1, sub, LANES))],
        out_specs=pl.BlockSpec((ts, c), tok),
        out_shape=jax.ShapeDtypeStruct((t, c), BF16),
        scratch_shapes=[pltpu.VMEM((ts + CONV_HALO, sub, LANES), F32),
                        pltpu.VMEM((ts, sub, LANES), F32)],
        compiler_params=_params(("arbitrary", "arbitrary")),
        name="dwconv",
    )(u, w.reshape(CONV_K, sub, LANES), b.reshape(1, sub, LANES))


def _gla_constants():
    c = GLA_CHUNK
    t = np.arange(c)[:, None]
    u = np.arange(c)[None, :]
    mats = np.zeros((GLA_LEVELS + 1, c, c), np.float32)
    mats[0] = u <= t
    for lvl in range(GLA_LEVELS):
        m = 1 << lvl
        pos = t % (2 * m)
        r = t - pos + m - 1
        mats[1 + lvl] = ((pos >= m) & (u > r) & (u <= t)) | ((pos < m) & (u > t) & (u <= r))
    x = t ^ u
    lv = np.where(x > 0, np.floor(np.log2(np.maximum(x, 1))).astype(np.int32), GLA_LEVELS)
    lv = np.where(u > t, -1, lv).astype(np.int32)
    return mats.reshape((GLA_LEVELS + 1) * c, c), lv


def _gla_kernel(q_ref, k_ref, v_ref, al_ref, w2_ref, ba_ref, gz_ref, ng_ref, mats_ref, lv_ref,
                o_ref, st_ref):
    c = GLA_CHUNK

    @pl.when(pl.program_id(1) == 0)
    def _():
        st_ref[...] = jnp.zeros_like(st_ref)

    al = al_ref[...]
    lane = lax.broadcasted_iota(jnp.int32, al.shape, 1)
    gate = jnp.where(lane < GLA_RANK, al, 0.0)
    hi = gate.astype(BF16).astype(F32)
    lo = gate - hi
    lhs = (hi + pltpu.roll(lo, GLA_RANK, 1) + pltpu.roll(hi, 2 * GLA_RANK, 1)).astype(BF16)
    x = _dot(lhs, w2_ref[...]) + ba_ref[...]
    log_a = (jnp.minimum(x, 0.0) - jnp.log1p(jnp.exp(-jnp.abs(x)))) * (1.0 / GLA_TAU)
    la = (log_a * LOG2_E).astype(BF16)

    lv = lv_ref[...]
    for hd in range(GLA_HEADS):
        ks = slice(hd * GLA_DKH, (hd + 1) * GLA_DKH)
        vs = slice(hd * GLA_DVH, (hd + 1) * GLA_DVH)
        q = q_ref[:, ks]
        k = k_ref[:, ks]
        v = v_ref[:, vs]

        e_all = _dot(mats_ref[...], la[:, ks])
        b2 = e_all[:c]
        b2_last = b2[c - 1:c, :]
        q_in = q * jnp.exp2(b2).astype(BF16)
        k_up = k * jnp.exp2(b2_last - b2).astype(BF16)

        st = st_ref[hd]
        o = _dot_nt(q_in, st.astype(BF16))

        p = jnp.where(lv == GLA_LEVELS, _dot_nt(q, k).astype(BF16), jnp.zeros((), BF16))
        for lvl in range(GLA_LEVELS):
            w = jnp.exp2(e_all[(1 + lvl) * c:(2 + lvl) * c]).astype(BF16)
            p = jnp.where(lv == lvl, _dot_nt(q * w, k * w).astype(BF16), p)
        o = o + _dot(p, v)

        st_ref[hd] = st * jnp.exp2(b2_last) + _dot_tn(v, k_up)

        y = o * _rms_scale(o) * ng_ref[:, vs]
        o_ref[:, vs] = (y * gz_ref[:, vs].astype(F32)).astype(o_ref.dtype)


def _gla(q, k, v, alpha, w2_cat, b_alpha, gz, norm_g, batch, seq):
    t = q.shape[0]
    c = GLA_CHUNK
    n_c = seq // c
    mats, lv = _gla_constants()
    tok = lambda bi, ni: (bi * n_c + ni, 0)
    return pl.pallas_call(
        _gla_kernel,
        grid=(batch, n_c),
        in_specs=[pl.BlockSpec((c, GLA_DK), tok), pl.BlockSpec((c, GLA_DK), tok),
                  pl.BlockSpec((c, GLA_DV), tok), pl.BlockSpec((c, RANK_PAD), tok),
                  _fixed((RANK_PAD, GLA_DK)), _fixed((1, GLA_DK)),
                  pl.BlockSpec((c, GLA_DV), tok), _fixed((1, GLA_DV)),
                  _fixed(mats.shape), _fixed(lv.shape)],
        out_specs=pl.BlockSpec((c, GLA_DV), tok),
        out_shape=jax.ShapeDtypeStruct((t, GLA_DV), BF16),
        scratch_shapes=[pltpu.VMEM((GLA_HEADS, GLA_DVH, GLA_DKH), F32)],
        compiler_params=_params(("arbitrary", "arbitrary")),
        name="gla",
    )(q, k, v, alpha, w2_cat, b_alpha.reshape(1, GLA_DK), gz, norm_g.reshape(1, GLA_DV),
      jnp.asarray(mats, BF16), jnp.asarray(lv, BF16))


def _mem_kv_kernel(m_ref, g_ref, w_ref, o_ref):
    m = m_ref[...]
    mn = (m * _rms_scale(m) * g_ref[...]).astype(BF16)
    o_ref[...] = _dot(mn, w_ref[...]).astype(o_ref.dtype)


def _mem_kv(mem2, g, w, batch, n_mem):
    d = mem2.shape[1]
    n = w.shape[1]
    return pl.pallas_call(
        _mem_kv_kernel,
        grid=(batch,),
        in_specs=[pl.BlockSpec((n_mem, d), lambda bi: (bi, 0)), _fixed((1, d)), _fixed((d, n))],
        out_specs=pl.BlockSpec((n_mem, n), lambda bi: (bi, 0)),
        out_shape=jax.ShapeDtypeStruct((batch * n_mem, n), BF16),
        compiler_params=_params(("parallel",)),
        name="mem_kv",
    )(mem2, g.reshape(1, d), w)


def _branches_kernel(c_ref, csz_ref, og_ref, mq_ref, msz_ref, kv_ref, g_ref,
                     lg_ref, lb_ref, wc_ref, bc_ref, wg_ref, wm_ref, y_ref):
    d = D_MODEL
    c = c_ref[...].astype(F32)
    dev = c - jnp.mean(c, axis=-1, keepdims=True)
    var = jnp.mean(dev * dev, axis=-1, keepdims=True)
    yh = dev * lax.rsqrt(var + EPS) * lg_ref[...] + lb_ref[...]
    u = _silu(yh) * csz_ref[...].astype(F32)
    y = (_dot(u.astype(BF16), wc_ref[...]) + bc_ref[...]) * g_ref[:, 0:d].astype(F32)

    y = y + _dot(og_ref[...], wg_ref[...]) * g_ref[:, d:2 * d].astype(F32)

    q = mq_ref[...]
    kv = kv_ref[...]
    outs = []
    for hd in range(MEM_HEADS):
        lo, hi = hd * MEM_HD, (hd + 1) * MEM_HD
        sc = _dot_nt(q[:, lo:hi], kv[:, lo:hi]) * (MEM_HD ** -0.5)
        e = jnp.exp(sc - jnp.max(sc, axis=-1, keepdims=True))
        p = e / jnp.sum(e, axis=-1, keepdims=True)
        outs.append(_dot(p.astype(BF16), kv[:, MEM_WIDTH + lo:MEM_WIDTH + hi]))
    om = jnp.concatenate(outs, axis=-1) * msz_ref[...].astype(F32)
    y = y + _dot(om.astype(BF16), wm_ref[...]) * g_ref[:, 2 * d:3 * d].astype(F32)
    y_ref[...] = y.astype(y_ref.dtype)


def _branches(conv, conv_sz, og, mq, mem_sz, mkv, gates, ln_g, ln_b, w_conv, b_conv, w_gla,
              w_mem, batch, seq, n_mem, tm):
    t, d = conv.shape
    n_t = seq // tm
    tok = lambda bi, ti: (bi * n_t + ti, 0)
    wide = lambda w: pl.BlockSpec((tm, w), tok)
    return pl.pallas_call(
        _branches_kernel,
        grid=(batch, n_t),
        in_specs=[wide(d), wide(d), wide(GLA_DV), wide(MEM_WIDTH), wide(MEM_WIDTH),
                  pl.BlockSpec((n_mem, 2 * MEM_WIDTH), lambda bi, ti: (bi, 0)),
                  wide(N_BRANCH * d),
                  _fixed((1, d)), _fixed((1, d)), _fixed((d, d)), _fixed((1, d)),
                  _fixed((GLA_DV, d)), _fixed((MEM_WIDTH, d))],
        out_specs=wide(d),
        out_shape=jax.ShapeDtypeStruct((t, d), BF16),
        compiler_params=_params(("parallel", "parallel")),
        name="branches",
    )(conv, conv_sz, og, mq, mem_sz, mkv, gates, ln_g.reshape(1, d), ln_b.reshape(1, d),
      w_conv, b_conv.reshape(1, d), w_gla, w_mem)


def _out_kernel(y_ref, x_ref, w_ref, g_ref, o_ref):
    r = x_ref[...] + _dot(y_ref[...], w_ref[...])
    o_ref[...] = r * _rms_scale(r) * g_ref[...]


def _out_proj(y, x, w, g, tm):
    t, d = x.shape
    row = lambda i: (i, 0)
    return pl.pallas_call(
        _out_kernel,
        grid=(t // tm,),
        in_specs=[pl.BlockSpec((tm, d), row), pl.BlockSpec((tm, d), row), _fixed((d, d)),
                  _fixed((1, d))],
        out_specs=pl.BlockSpec((tm, d), row),
        out_shape=jax.ShapeDtypeStruct((t, d), F32),
        compiler_params=_params(("parallel",)),
        name="out_proj",
    )(y, x, w, g.reshape(1, d))


def kernel(x, mem, ln_in_g, mem_ln_g, w_in, b_gate, dw_w, dw_b, conv_ln_g, conv_ln_b,
           w_conv_out, b_conv_out, w_alpha2, b_alpha, gla_norm_g, w_gla_out,
           w_mem_kv, w_mem_out, w_out, final_g):
    batch, seq, d = x.shape
    n_mem = mem.shape[1]
    depth = w_in.shape[0]
    assert depth == 1 and d == D_MODEL
    t = batch * seq
    xs = x.reshape(t, d)
    mem2 = mem.reshape(batch * n_mem, d)
    bf = lambda a: a.astype(BF16)
    l = 0

    widths = [d, d, d, GLA_DK, GLA_DK, GLA_DV, GLA_DV, GLA_RANK, MEM_WIDTH, MEM_WIDTH, N_BRANCH * d]
    offs = [int(o) for o in np.cumsum([0] + widths)]
    (o_ca, o_cb, o_cz, o_q, o_k, o_v, o_gz, o_al, o_mq, o_mz, o_g, o_end) = offs
    wt = w_in[l].T
    w2 = w_alpha2[l]
    w2_hi = bf(w2)
    w2_lo = bf(w2 - w2_hi.astype(F32))
    w2_cat = jnp.zeros((RANK_PAD, GLA_DK), BF16).at[:3 * GLA_RANK].set(
        jnp.concatenate([w2_hi, w2_hi, w2_lo], axis=0))

    w_qk, w_m = lax.optimization_barrier((wt[o_q:o_v], wt[o_al:o_g]))
    h, q, k, mq, mem_sz, alpha = _norm_qkm(xs, ln_in_g[l], bf(w_qk), bf(w_m), tm=512)

    u, conv_sz = _proj_call(_proj_conv_kernel, "proj_conv", h, wt, [o_ca, o_cb, o_cz], d,
                            [BF16, BF16], tm=1024, tn=512)
    v, gla_sz = _proj_call(_proj_vz_kernel, "proj_vz", h, wt, [o_v, o_gz], GLA_DV,
                           [BF16, BF16], tm=1024, tn=512)
    (gates,) = _proj_call(_proj_gate_kernel, "proj_gate", h, wt, [o_g], N_BRANCH * d, [BF16],
                          tm=1024, tn=1024, bias=b_gate[l])

    conv = _dwconv(u, dw_w[l], dw_b[l], batch, seq, ts=512)
    og = _gla(q, k, v, alpha, w2_cat, b_alpha[l], gla_sz, gla_norm_g[l], batch, seq)
    mkv = _mem_kv(mem2, mem_ln_g[l], bf(w_mem_kv[l]), batch, n_mem)

    y = _branches(conv, conv_sz, og, mq, mem_sz, mkv, gates, conv_ln_g[l], conv_ln_b[l],
                  bf(w_conv_out[l]), b_conv_out[l], bf(w_gla_out[l]), bf(w_mem_out[l]),
                  batch, seq, n_mem, tm=512)
    out = _out_proj(y, xs, bf(w_out[l]), final_g, tm=512)
    return out.reshape(batch, seq, d)
```

```python
import functools

import numpy as np
import jax
import jax.numpy as jnp
from jax import lax
from jax.experimental import pallas as pl
from jax.experimental.pallas import tpu as pltpu

F32 = jnp.float32
BF16 = jnp.bfloat16

D_MODEL = 2048
EPS = 1e-6
CONV_K = 31
GLA_HEADS = 4
GLA_DK = 1024
GLA_DV = 2048
GLA_DKH = GLA_DK // GLA_HEADS
GLA_DVH = GLA_DV // GLA_HEADS
GLA_RANK = 16
GLA_TAU = 16.0
MEM_HEADS = 4
MEM_HD = 128
MEM_WIDTH = MEM_HEADS * MEM_HD
N_BRANCH = 3

LANES = 128
VMEM_LIMIT = 60 * 1024 * 1024

GLA_CHUNK = 256
GLA_LEVELS = 8
RANK_PAD = LANES
CONV_HALO = 32
CONV_TB = 16
LOG2_E = 1.4426950408889634
W_ROW_ALIGN = 16


def _params(sem):
    return pltpu.CompilerParams(dimension_semantics=sem, vmem_limit_bytes=VMEM_LIMIT)


def _fixed(shape, index=None):
    index = (0,) * len(shape) if index is None else tuple(index)
    return pl.BlockSpec(shape, lambda *_: index, pipeline_mode=pl.Buffered(1))


def _sigmoid(x):
    return 1.0 / (1.0 + jnp.exp(-x))


def _silu(x):
    return x * _sigmoid(x)


def _dot(a, b):
    return jnp.dot(a, b, preferred_element_type=F32)


def _dot_nt(a, b):
    return lax.dot_general(a, b, (((1,), (1,)), ((), ())), preferred_element_type=F32)


def _dot_tn(a, b):
    return lax.dot_general(a, b, (((0,), (0,)), ((), ())), preferred_element_type=F32)


def _rms_scale(x):
    return lax.rsqrt(jnp.mean(x * x, axis=-1, keepdims=True) + EPS)


def _round_weights_once(wt_refs, wb_ref):
    @pl.when(pl.program_id(1) == 0)
    def _():
        for n, wt_ref in enumerate(wt_refs):
            wb_ref[n] = wt_ref[...].T.astype(BF16)


def _proj_conv_kernel(h_ref, wa_ref, wb_ref, wz_ref, u_ref, sz_ref, w_ref):
    _round_weights_once((wa_ref, wb_ref, wz_ref), w_ref)
    h = h_ref[...]
    u_ref[...] = (_dot(h, w_ref[0]) * _sigmoid(_dot(h, w_ref[1]))).astype(BF16)
    sz_ref[...] = _silu(_dot(h, w_ref[2])).astype(BF16)


def _proj_vz_kernel(h_ref, wv_ref, wz_ref, v_ref, sz_ref, w_ref):
    _round_weights_once((wv_ref, wz_ref), w_ref)
    h = h_ref[...]
    v_ref[...] = _dot(h, w_ref[0]).astype(BF16)
    sz_ref[...] = _silu(_dot(h, w_ref[1])).astype(BF16)


def _proj_gate_kernel(h_ref, wg_ref, b_ref, g_ref, w_ref):
    _round_weights_once((wg_ref,), w_ref)
    g_ref[...] = _sigmoid(_dot(h_ref[...], w_ref[0]) + b_ref[...]).astype(BF16)


def _wt_rows(rows, d, index_map, buffers):
    def im(*idx):
        return pl.multiple_of(index_map(*idx), W_ROW_ALIGN), 0
    return pl.BlockSpec((pl.Element(rows), pl.Element(d)), im, pipeline_mode=pl.Buffered(buffers))


def _proj_call(kernel, name, h, wt, row_offsets, width, out_dtypes, tm, tn, bias=None,
               weight_buffers=2):
    t, d = h.shape
    assert all(off % W_ROW_ALIGN == 0 for off in row_offsets) and width % tn == 0
    in_specs = [pl.BlockSpec((tm, d), lambda j, i: (i, 0))]
    in_specs += [_wt_rows(tn, d, lambda j, i, o=off: o + j * tn, weight_buffers)
                 for off in row_offsets]
    args = [h] + [wt] * len(row_offsets)
    if bias is not None:
        in_specs.append(pl.BlockSpec((1, tn), lambda j, i: (0, j)))
        args.append(bias.reshape(1, width))
    return pl.pallas_call(
        kernel,
        grid=(width // tn, t // tm),
        in_specs=in_specs,
        out_specs=[pl.BlockSpec((tm, tn), lambda j, i: (i, j)) for _ in out_dtypes],
        out_shape=[jax.ShapeDtypeStruct((t, width), dt) for dt in out_dtypes],
        scratch_shapes=[pltpu.VMEM((len(row_offsets), d, tn), BF16)],
        compiler_params=_params(("arbitrary", "arbitrary")),
        name=name,
    )(*args)


def _norm_qkm_kernel(x_ref, g_ref, wqk_ref, wm_ref, h_ref, q_ref, k_ref, mq_ref, sz_ref, al_ref):
    x = x_ref[...]
    h = (x * _rms_scale(x) * g_ref[...]).astype(BF16)
    h_ref[...] = h
    al_ref[...] = _dot_nt(h, wm_ref[0:RANK_PAD])
    q_ref[...] = (_dot_nt(h, wqk_ref[0:GLA_DK]) * (GLA_DKH ** -0.5)).astype(BF16)
    k_ref[...] = _dot_nt(h, wqk_ref[GLA_DK:2 * GLA_DK]).astype(BF16)
    mq_ref[...] = _dot_nt(h, wm_ref[GLA_RANK:GLA_RANK + MEM_WIDTH]).astype(BF16)
    sz_ref[...] = _silu(_dot_nt(h, wm_ref[GLA_RANK + MEM_WIDTH:GLA_RANK + 2 * MEM_WIDTH])).astype(BF16)


def _norm_qkm(x, g, w_qk, w_m, tm):
    t, d = x.shape
    row = lambda i: (i, 0)
    out = lambda n: pl.BlockSpec((tm, n), row)
    return pl.pallas_call(
        _norm_qkm_kernel,
        grid=(t // tm,),
        in_specs=[pl.BlockSpec((tm, d), row), _fixed((1, d)), _fixed(w_qk.shape), _fixed(w_m.shape)],
        out_specs=[out(d), out(GLA_DK), out(GLA_DK), out(MEM_WIDTH), out(MEM_WIDTH), out(RANK_PAD)],
        out_shape=[jax.ShapeDtypeStruct((t, d), BF16),
                   jax.ShapeDtypeStruct((t, GLA_DK), BF16), jax.ShapeDtypeStruct((t, GLA_DK), BF16),
                   jax.ShapeDtypeStruct((t, MEM_WIDTH), BF16),
                   jax.ShapeDtypeStruct((t, MEM_WIDTH), BF16),
                   jax.ShapeDtypeStruct((t, RANK_PAD), F32)],
        compiler_params=_params(("parallel",)),
        name="norm_qkm",
    )(x, g.reshape(1, d), w_qk, w_m)


def _dwconv_kernel(u_ref, w_ref, b_ref, o_ref, ext_ref, acc_ref, *, ts):
    s = pl.program_id(1)
    sub = ext_ref.shape[1]

    @pl.when(s == 0)
    def _():
        ext_ref[0:CONV_HALO] = jnp.zeros((CONV_HALO,) + ext_ref.shape[1:], F32)

    @pl.when(s > 0)
    def _():
        ext_ref[0:CONV_HALO] = ext_ref[ts:ts + CONV_HALO]

    ext_ref[CONV_HALO:CONV_HALO + ts] = pltpu.einshape(
        "t(sl)->tsl", u_ref[...].astype(F32), s=sub)
    lead = CONV_HALO - (CONV_K - 1)

    def body(i, carry):
        t0 = pl.multiple_of(i * CONV_TB, CONV_TB)
        acc = jnp.broadcast_to(b_ref[...], (CONV_TB,) + ext_ref.shape[1:])
        for j in range(CONV_K):
            acc = acc + w_ref[j] * ext_ref[pl.ds(t0 + lead + j, CONV_TB)]
        acc_ref[pl.ds(t0, CONV_TB)] = acc
        return carry

    lax.fori_loop(0, ts // CONV_TB, body, 0)
    o_ref[...] = pltpu.einshape("tsl->t(sl)", acc_ref[...]).astype(o_ref.dtype)


def _dwconv(u, w, b, batch, seq, ts):
    t, c = u.shape
    sub = c // LANES
    n_s = seq // ts
    tok = lambda bi, si: (bi * n_s + si, 0)
    return pl.pallas_call(
        functools.partial(_dwconv_kernel, ts=ts),
        grid=(batch, n_s),
        in_specs=[pl.BlockSpec((ts, c), tok), _fixed((CONV_K, sub, LANES)), _fixed((1, sub, LANES))],
        out_specs=pl.BlockSpec((ts, c), tok),
        out_shape=jax.ShapeDtypeStruct((t, c), BF16),
        scratch_shapes=[pltpu.VMEM((ts + CONV_HALO, sub, LANES), F32),
                        pltpu.VMEM((ts, sub, LANES), F32)],
        compiler_params=_params(("arbitrary", "arbitrary")),
        name="dwconv",
    )(u, w.reshape(CONV_K, sub, LANES), b.reshape(1, sub, LANES))


def _gla_constants():
    c = GLA_CHUNK
    t = np.arange(c)[:, None]
    u = np.arange(c)[None, :]
    mats = np.zeros((GLA_LEVELS + 1, c, c), np.float32)
    mats[0] = u <= t
    for lvl in range(GLA_LEVELS):
        m = 1 << lvl
        pos = t % (2 * m)
        r = t - pos + m - 1
        mats[1 + lvl] = ((pos >= m) & (u > r) & (u <= t)) | ((pos < m) & (u > t) & (u <= r))
    x = t ^ u
    lv = np.where(x > 0, np.floor(np.log2(np.maximum(x, 1))).astype(np.int32), GLA_LEVELS)
    lv = np.where(u > t, -1, lv).astype(np.int32)
    return mats.reshape((GLA_LEVELS + 1) * c, c), lv


def _gla_kernel(q_ref, k_ref, v_ref, al_ref, w2_ref, ba_ref, gz_ref, ng_ref, mats_ref, lv_ref,
                o_ref, st_ref):
    c = GLA_CHUNK

    @pl.when(pl.program_id(1) == 0)
    def _():
        st_ref[...] = jnp.zeros_like(st_ref)

    al = al_ref[...]
    lane = lax.broadcasted_iota(jnp.int32, al.shape, 1)
    gate = jnp.where(lane < GLA_RANK, al, 0.0)
    hi = gate.astype(BF16).astype(F32)
    lo = gate - hi
    lhs = (hi + pltpu.roll(lo, GLA_RANK, 1) + pltpu.roll(hi, 2 * GLA_RANK, 1)).astype(BF16)
    x = _dot(lhs, w2_ref[...]) + ba_ref[...]
    log_a = (jnp.minimum(x, 0.0) - jnp.log1p(jnp.exp(-jnp.abs(x)))) * (1.0 / GLA_TAU)
    la = (log_a * LOG2_E).astype(BF16)

    lv = lv_ref[...]
    for hd in range(GLA_HEADS):
        ks = slice(hd * GLA_DKH, (hd + 1) * GLA_DKH)
        vs = slice(hd * GLA_DVH, (hd + 1) * GLA_DVH)
        q = q_ref[:, ks]
        k = k_ref[:, ks]
        v = v_ref[:, vs]

        e_all = _dot(mats_ref[...], la[:, ks])
        b2 = e_all[:c]
        b2_last = b2[c - 1:c, :]
        q_in = q * jnp.exp2(b2).astype(BF16)
        k_up = k * jnp.exp2(b2_last - b2).astype(BF16)

        st = st_ref[hd]
        o = _dot_nt(q_in, st.astype(BF16))

        p = jnp.where(lv == GLA_LEVELS, _dot_nt(q, k).astype(BF16), jnp.zeros((), BF16))
        for lvl in range(GLA_LEVELS):
            w = jnp.exp2(e_all[(1 + lvl) * c:(2 + lvl) * c]).astype(BF16)
            p = jnp.where(lv == lvl, _dot_nt(q * w, k * w).astype(BF16), p)
        o = o + _dot(p, v)

        st_ref[hd] = st * jnp.exp2(b2_last) + _dot_tn(v, k_up)

        y = o * _rms_scale(o) * ng_ref[:, vs]
        o_ref[:, vs] = (y * gz_ref[:, vs].astype(F32)).astype(o_ref.dtype)


def _gla(q, k, v, alpha, w2_cat, b_alpha, gz, norm_g, batch, seq):
    t = q.shape[0]
    c = GLA_CHUNK
    n_c = seq // c
    mats, lv = _gla_constants()
    tok = lambda bi, ni: (bi * n_c + ni, 0)
    return pl.pallas_call(
        _gla_kernel,
        grid=(batch, n_c),
        in_specs=[pl.BlockSpec((c, GLA_DK), tok), pl.BlockSpec((c, GLA_DK), tok),
                  pl.BlockSpec((c, GLA_DV), tok), pl.BlockSpec((c, RANK_PAD), tok),
                  _fixed((RANK_PAD, GLA_DK)), _fixed((1, GLA_DK)),
                  pl.BlockSpec((c, GLA_DV), tok), _fixed((1, GLA_DV)),
                  _fixed(mats.shape), _fixed(lv.shape)],
        out_specs=pl.BlockSpec((c, GLA_DV), tok),
        out_shape=jax.ShapeDtypeStruct((t, GLA_DV), BF16),
        scratch_shapes=[pltpu.VMEM((GLA_HEADS, GLA_DVH, GLA_DKH), F32)],
        compiler_params=_params(("arbitrary", "arbitrary")),
        name="gla",
    )(q, k, v, alpha, w2_cat, b_alpha.reshape(1, GLA_DK), gz, norm_g.reshape(1, GLA_DV),
      jnp.asarray(mats, BF16), jnp.asarray(lv, BF16))


def _mem_kv_kernel(m_ref, g_ref, w_ref, o_ref):
    m = m_ref[...]
    mn = (m * _rms_scale(m) * g_ref[...]).astype(BF16)
    o_ref[...] = _dot(mn, w_ref[...]).astype(o_ref.dtype)


def _mem_kv(mem2, g, w, batch, n_mem):
    d = mem2.shape[1]
    n = w.shape[1]
    return pl.pallas_call(
        _mem_kv_kernel,
        grid=(batch,),
        in_specs=[pl.BlockSpec((n_mem, d), lambda bi: (bi, 0)), _fixed((1, d)), _fixed((d, n))],
        out_specs=pl.BlockSpec((n_mem, n), lambda bi: (bi, 0)),
        out_shape=jax.ShapeDtypeStruct((batch * n_mem, n), BF16),
        compiler_params=_params(("parallel",)),
        name="mem_kv",
    )(mem2, g.reshape(1, d), w)


def _branches_kernel(c_ref, csz_ref, og_ref, mq_ref, msz_ref, kv_ref, g_ref,
                     lg_ref, lb_ref, wc_ref, bc_ref, wg_ref, wm_ref, y_ref):
    d = D_MODEL
    c = c_ref[...].astype(F32)
    dev = c - jnp.mean(c, axis=-1, keepdims=True)
    var = jnp.mean(dev * dev, axis=-1, keepdims=True)
    yh = dev * lax.rsqrt(var + EPS) * lg_ref[...] + lb_ref[...]
    u = _silu(yh) * csz_ref[...].astype(F32)
    y = (_dot(u.astype(BF16), wc_ref[...]) + bc_ref[...]) * g_ref[:, 0:d].astype(F32)

    y = y + _dot(og_ref[...], wg_ref[...]) * g_ref[:, d:2 * d].astype(F32)

    q = mq_ref[...]
    kv = kv_ref[...]
    outs = []
    for hd in range(MEM_HEADS):
        lo, hi = hd * MEM_HD, (hd + 1) * MEM_HD
        sc = _dot_nt(q[:, lo:hi], kv[:, lo:hi]) * (MEM_HD ** -0.5)
        e = jnp.exp(sc - jnp.max(sc, axis=-1, keepdims=True))
        p = e / jnp.sum(e, axis=-1, keepdims=True)
        outs.append(_dot(p.astype(BF16), kv[:, MEM_WIDTH + lo:MEM_WIDTH + hi]))
    om = jnp.concatenate(outs, axis=-1) * msz_ref[...].astype(F32)
    y = y + _dot(om.astype(BF16), wm_ref[...]) * g_ref[:, 2 * d:3 * d].astype(F32)
    y_ref[...] = y.astype(y_ref.dtype)


def _branches(conv, conv_sz, og, mq, mem_sz, mkv, gates, ln_g, ln_b, w_conv, b_conv, w_gla,
              w_mem, batch, seq, n_mem, tm):
    t, d = conv.shape
    n_t = seq // tm
    tok = lambda bi, ti: (bi * n_t + ti, 0)
    wide = lambda w: pl.BlockSpec((tm, w), tok)
    return pl.pallas_call(
        _branches_kernel,
        grid=(batch, n_t),
        in_specs=[wide(d), wide(d), wide(GLA_DV), wide(MEM_WIDTH), wide(MEM_WIDTH),
                  pl.BlockSpec((n_mem, 2 * MEM_WIDTH), lambda bi, ti: (bi, 0)),
                  wide(N_BRANCH * d),
                  _fixed((1, d)), _fixed((1, d)), _fixed((d, d)), _fixed((1, d)),
                  _fixed((GLA_DV, d)), _fixed((MEM_WIDTH, d))],
        out_specs=wide(d),
        out_shape=jax.ShapeDtypeStruct((t, d), BF16),
        compiler_params=_params(("parallel", "parallel")),
        name="branches",
    )(conv, conv_sz, og, mq, mem_sz, mkv, gates, ln_g.reshape(1, d), ln_b.reshape(1, d),
      w_conv, b_conv.reshape(1, d), w_gla, w_mem)


def _out_kernel(y_ref, x_ref, w_ref, g_ref, o_ref):
    r = x_ref[...] + _dot(y_ref[...], w_ref[...])
    o_ref[...] = r * _rms_scale(r) * g_ref[...]


def _out_proj(y, x, w, g, tm):
    t, d = x.shape
    row = lambda i: (i, 0)
    return pl.pallas_call(
        _out_kernel,
        grid=(t // tm,),
        in_specs=[pl.BlockSpec((tm, d), row), pl.BlockSpec((tm, d), row), _fixed((d, d)),
                  _fixed((1, d))],
        out_specs=pl.BlockSpec((tm, d), row),
        out_shape=jax.ShapeDtypeStruct((t, d), F32),
        compiler_params=_params(("parallel",)),
        name="out_proj",
    )(y, x, w, g.reshape(1, d))


def kernel(x, mem, ln_in_g, mem_ln_g, w_in, b_gate, dw_w, dw_b, conv_ln_g, conv_ln_b,
           w_conv_out, b_conv_out, w_alpha2, b_alpha, gla_norm_g, w_gla_out,
           w_mem_kv, w_mem_out, w_out, final_g):
    batch, seq, d = x.shape
    n_mem = mem.shape[1]
    depth = w_in.shape[0]
    assert depth == 1 and d == D_MODEL
    t = batch * seq
    xs = x.reshape(t, d)
    mem2 = mem.reshape(batch * n_mem, d)
    bf = lambda a: a.astype(BF16)
    l = 0

    widths = [d, d, d, GLA_DK, GLA_DK, GLA_DV, GLA_DV, GLA_RANK, MEM_WIDTH, MEM_WIDTH, N_BRANCH * d]
    offs = [int(o) for o in np.cumsum([0] + widths)]
    (o_ca, o_cb, o_cz, o_q, o_k, o_v, o_gz, o_al, o_mq, o_mz, o_g, o_end) = offs
    wt = w_in[l].T
    w2 = w_alpha2[l]
    w2_hi = bf(w2)
    w2_lo = bf(w2 - w2_hi.astype(F32))
    w2_cat = jnp.zeros((RANK_PAD, GLA_DK), BF16).at[:3 * GLA_RANK].set(
        jnp.concatenate([w2_hi, w2_hi, w2_lo], axis=0))

    w_qk, w_m = lax.optimization_barrier((wt[o_q:o_v], wt[o_al:o_g]))
    h, q, k, mq, mem_sz, alpha = _norm_qkm(xs, ln_in_g[l], bf(w_qk), bf(w_m), tm=512)

    u, conv_sz = _proj_call(_proj_conv_kernel, "proj_conv", h, wt, [o_ca, o_cb, o_cz], d,
                            [BF16, BF16], tm=1024, tn=512)
    v, gla_sz = _proj_call(_proj_vz_kernel, "proj_vz", h, wt, [o_v, o_gz], GLA_DV,
                           [BF16, BF16], tm=1024, tn=1024, weight_buffers=1)
    (gates,) = _proj_call(_proj_gate_kernel, "proj_gate", h, wt, [o_g], N_BRANCH * d, [BF16],
                          tm=1024, tn=2048, bias=b_gate[l], weight_buffers=1)

    conv = _dwconv(u, dw_w[l], dw_b[l], batch, seq, ts=512)
    og = _gla(q, k, v, alpha, w2_cat, b_alpha[l], gla_sz, gla_norm_g[l], batch, seq)
    mkv = _mem_kv(mem2, mem_ln_g[l], bf(w_mem_kv[l]), batch, n_mem)

    y = _branches(conv, conv_sz, og, mq, mem_sz, mkv, gates, conv_ln_g[l], conv_ln_b[l],
                  bf(w_conv_out[l]), b_conv_out[l], bf(w_gla_out[l]), bf(w_mem_out[l]),
                  batch, seq, n_mem, tm=512)
    out = _out_proj(y, xs, bf(w_out[l]), final_g, tm=512)
    return out.reshape(batch, seq, d)
```

```python
import functools

import numpy as np
import jax
import jax.numpy as jnp
from jax import lax
from jax.experimental import pallas as pl
from jax.experimental.pallas import tpu as pltpu

F32 = jnp.float32
BF16 = jnp.bfloat16

D_MODEL = 2048
EPS = 1e-6
CONV_K = 31
GLA_HEADS = 4
GLA_DK = 1024
GLA_DV = 2048
GLA_DKH = GLA_DK // GLA_HEADS
GLA_DVH = GLA_DV // GLA_HEADS
GLA_RANK = 16
GLA_TAU = 16.0
MEM_HEADS = 4
MEM_HD = 128
MEM_WIDTH = MEM_HEADS * MEM_HD
N_BRANCH = 3

LANES = 128
VMEM_LIMIT = 60 * 1024 * 1024

GLA_CHUNK = 256
GLA_LEVELS = 8
RANK_PAD = LANES
CONV_HALO = 32
CONV_TB = 16
LOG2_E = 1.4426950408889634
W_ROW_ALIGN = 16

TILES = dict(
    norm_qkm=dict(tm=512),
    proj_conv=dict(tm=1024, tn=512, weight_buffers=2),
    proj_vz=dict(tm=1024, tn=1024, weight_buffers=1),
    proj_gate=dict(tm=1024, tn=1024, weight_buffers=2),
    dwconv=dict(ts=1024),
    branches=dict(tm=512),
    out_proj=dict(tm=1024),
)


def _params(sem):
    return pltpu.CompilerParams(dimension_semantics=sem, vmem_limit_bytes=VMEM_LIMIT)


def _fixed(shape, index=None):
    index = (0,) * len(shape) if index is None else tuple(index)
    return pl.BlockSpec(shape, lambda *_: index, pipeline_mode=pl.Buffered(1))


def _sigmoid(x):
    return 1.0 / (1.0 + jnp.exp(-x))


def _silu(x):
    return x * _sigmoid(x)


def _dot(a, b):
    return jnp.dot(a, b, preferred_element_type=F32)


def _dot_nt(a, b):
    return lax.dot_general(a, b, (((1,), (1,)), ((), ())), preferred_element_type=F32)


def _dot_tn(a, b):
    return lax.dot_general(a, b, (((0,), (0,)), ((), ())), preferred_element_type=F32)


def _rms_scale(x):
    return lax.rsqrt(jnp.mean(x * x, axis=-1, keepdims=True) + EPS)


def _round_weights_once(wt_refs, wb_ref):
    @pl.when(pl.program_id(1) == 0)
    def _():
        for n, wt_ref in enumerate(wt_refs):
            wb_ref[n] = wt_ref[...].T.astype(BF16)


def _proj_conv_kernel(h_ref, wa_ref, wb_ref, wz_ref, u_ref, sz_ref, w_ref):
    _round_weights_once((wa_ref, wb_ref, wz_ref), w_ref)
    h = h_ref[...]
    u_ref[...] = (_dot(h, w_ref[0]) * _sigmoid(_dot(h, w_ref[1]))).astype(BF16)
    sz_ref[...] = _silu(_dot(h, w_ref[2])).astype(BF16)


def _proj_vz_kernel(h_ref, wv_ref, wz_ref, v_ref, sz_ref, w_ref):
    _round_weights_once((wv_ref, wz_ref), w_ref)
    h = h_ref[...]
    v_ref[...] = _dot(h, w_ref[0]).astype(BF16)
    sz_ref[...] = _silu(_dot(h, w_ref[1])).astype(BF16)


def _proj_gate_kernel(h_ref, wg_ref, b_ref, g_ref, w_ref):
    _round_weights_once((wg_ref,), w_ref)
    g_ref[...] = _sigmoid(_dot(h_ref[...], w_ref[0]) + b_ref[...]).astype(BF16)


def _wt_rows(rows, d, index_map, buffers):
    def im(*idx):
        return pl.multiple_of(index_map(*idx), W_ROW_ALIGN), 0
    return pl.BlockSpec((pl.Element(rows), pl.Element(d)), im, pipeline_mode=pl.Buffered(buffers))


def _proj_call(kernel, name, h, wt, row_offsets, width, out_dtypes, tm, tn, bias=None,
               weight_buffers=2):
    t, d = h.shape
    assert all(off % W_ROW_ALIGN == 0 for off in row_offsets) and width % tn == 0
    in_specs = [pl.BlockSpec((tm, d), lambda j, i: (i, 0))]
    in_specs += [_wt_rows(tn, d, lambda j, i, o=off: o + j * tn, weight_buffers)
                 for off in row_offsets]
    args = [h] + [wt] * len(row_offsets)
    if bias is not None:
        in_specs.append(pl.BlockSpec((1, tn), lambda j, i: (0, j)))
        args.append(bias.reshape(1, width))
    return pl.pallas_call(
        kernel,
        grid=(width // tn, t // tm),
        in_specs=in_specs,
        out_specs=[pl.BlockSpec((tm, tn), lambda j, i: (i, j)) for _ in out_dtypes],
        out_shape=[jax.ShapeDtypeStruct((t, width), dt) for dt in out_dtypes],
        scratch_shapes=[pltpu.VMEM((len(row_offsets), d, tn), BF16)],
        compiler_params=_params(("arbitrary", "arbitrary")),
        name=name,
    )(*args)


def _norm_qkm_kernel(x_ref, g_ref, wqk_ref, wm_ref, h_ref, q_ref, k_ref, mq_ref, sz_ref, al_ref):
    x = x_ref[...]
    h = (x * _rms_scale(x) * g_ref[...]).astype(BF16)
    h_ref[...] = h
    al_ref[...] = _dot_nt(h, wm_ref[0:RANK_PAD])
    q_ref[...] = (_dot_nt(h, wqk_ref[0:GLA_DK]) * (GLA_DKH ** -0.5)).astype(BF16)
    k_ref[...] = _dot_nt(h, wqk_ref[GLA_DK:2 * GLA_DK]).astype(BF16)
    mq_ref[...] = _dot_nt(h, wm_ref[GLA_RANK:GLA_RANK + MEM_WIDTH]).astype(BF16)
    sz_ref[...] = _silu(_dot_nt(h, wm_ref[GLA_RANK + MEM_WIDTH:GLA_RANK + 2 * MEM_WIDTH])).astype(BF16)


def _norm_qkm(x, g, w_qk, w_m, tm):
    t, d = x.shape
    row = lambda i: (i, 0)
    out = lambda n: pl.BlockSpec((tm, n), row)
    return pl.pallas_call(
        _norm_qkm_kernel,
        grid=(t // tm,),
        in_specs=[pl.BlockSpec((tm, d), row), _fixed((1, d)), _fixed(w_qk.shape), _fixed(w_m.shape)],
        out_specs=[out(d), out(GLA_DK), out(GLA_DK), out(MEM_WIDTH), out(MEM_WIDTH), out(RANK_PAD)],
        out_shape=[jax.ShapeDtypeStruct((t, d), BF16),
                   jax.ShapeDtypeStruct((t, GLA_DK), BF16), jax.ShapeDtypeStruct((t, GLA_DK), BF16),
                   jax.ShapeDtypeStruct((t, MEM_WIDTH), BF16),
                   jax.ShapeDtypeStruct((t, MEM_WIDTH), BF16),
                   jax.ShapeDtypeStruct((t, RANK_PAD), F32)],
        compiler_params=_params(("parallel",)),
        name="norm_qkm",
    )(x, g.reshape(1, d), w_qk, w_m)


def _dwconv_kernel(u_ref, w_ref, b_ref, o_ref, ext_ref, acc_ref, *, ts):
    s = pl.program_id(1)
    sub = ext_ref.shape[1]

    @pl.when(s == 0)
    def _():
        ext_ref[0:CONV_HALO] = jnp.zeros((CONV_HALO,) + ext_ref.shape[1:], F32)

    @pl.when(s > 0)
    def _():
        ext_ref[0:CONV_HALO] = ext_ref[ts:ts + CONV_HALO]

    ext_ref[CONV_HALO:CONV_HALO + ts] = pltpu.einshape(
        "t(sl)->tsl", u_ref[...].astype(F32), s=sub)
    lead = CONV_HALO - (CONV_K - 1)

    def body(i, carry):
        t0 = pl.multiple_of(i * CONV_TB, CONV_TB)
        acc = jnp.broadcast_to(b_ref[...], (CONV_TB,) + ext_ref.shape[1:])
        for j in range(CONV_K):
            acc = acc + w_ref[j] * ext_ref[pl.ds(t0 + lead + j, CONV_TB)]
        acc_ref[pl.ds(t0, CONV_TB)] = acc
        return carry

    lax.fori_loop(0, ts // CONV_TB, body, 0)
    o_ref[...] = pltpu.einshape("tsl->t(sl)", acc_ref[...]).astype(o_ref.dtype)


def _dwconv(u, w, b, batch, seq, ts):
    t, c = u.shape
    sub = c // LANES
    n_s = seq // ts
    tok = lambda bi, si: (bi * n_s + si, 0)
    return pl.pallas_call(
        functools.partial(_dwconv_kernel, ts=ts),
        grid=(batch, n_s),
        in_specs=[pl.BlockSpec((ts, c), tok), _fixed((CONV_K, sub, LANES)), _fixed((1, sub, LANES))],
        out_specs=pl.BlockSpec((ts, c), tok),
        out_shape=jax.ShapeDtypeStruct((t, c), BF16),
        scratch_shapes=[pltpu.VMEM((ts + CONV_HALO, sub, LANES), F32),
                        pltpu.VMEM((ts, sub, LANES), F32)],
        compiler_params=_params(("arbitrary", "arbitrary")),
        name="dwconv",
    )(u, w.reshape(CONV_K, sub, LANES), b.reshape(1, sub, LANES))


def _gla_constants():
    c = GLA_CHUNK
    t = np.arange(c)[:, None]
    u = np.arange(c)[None, :]
    mats = np.zeros((GLA_LEVELS + 1, c, c), np.float32)
    mats[0] = u <= t
    for lvl in range(GLA_LEVELS):
        m = 1 << lvl
        pos = t % (2 * m)
        r = t - pos + m - 1
        mats[1 + lvl] = ((pos >= m) & (u > r) & (u <= t)) | ((pos < m) & (u > t) & (u <= r))
    x = t ^ u
    lv = np.where(x > 0, np.floor(np.log2(np.maximum(x, 1))).astype(np.int32), GLA_LEVELS)
    lv = np.where(u > t, -1, lv).astype(np.int32)
    return mats.reshape((GLA_LEVELS + 1) * c, c), lv


def _gla_kernel(q_ref, k_ref, v_ref, al_ref, w2_ref, ba_ref, gz_ref, ng_ref, mats_ref, lv_ref,
                o_ref, st_ref):
    c = GLA_CHUNK

    @pl.when(pl.program_id(1) == 0)
    def _():
        st_ref[...] = jnp.zeros_like(st_ref)

    al = al_ref[...]
    lane = lax.broadcasted_iota(jnp.int32, al.shape, 1)
    gate = jnp.where(lane < GLA_RANK, al, 0.0)
    hi = gate.astype(BF16).astype(F32)
    lo = gate - hi
    lhs = (hi + pltpu.roll(lo, GLA_RANK, 1) + pltpu.roll(hi, 2 * GLA_RANK, 1)).astype(BF16)
    x = _dot(lhs, w2_ref[...]) + ba_ref[...]
    log_a = (jnp.minimum(x, 0.0) - jnp.log1p(jnp.exp(-jnp.abs(x)))) * (1.0 / GLA_TAU)
    la = (log_a * LOG2_E).astype(BF16)

    lv = lv_ref[...]
    for hd in range(GLA_HEADS):
        ks = slice(hd * GLA_DKH, (hd + 1) * GLA_DKH)
        vs = slice(hd * GLA_DVH, (hd + 1) * GLA_DVH)
        q = q_ref[:, ks]
        k = k_ref[:, ks]
        v = v_ref[:, vs]

        e_all = _dot(mats_ref[...], la[:, ks])
        b2 = e_all[:c]
        b2_last = b2[c - 1:c, :]
        q_in = q * jnp.exp2(b2).astype(BF16)
        k_up = k * jnp.exp2(b2_last - b2).astype(BF16)

        st = st_ref[hd]
        o = _dot_nt(q_in, st.astype(BF16))

        p = jnp.where(lv == GLA_LEVELS, _dot_nt(q, k).astype(BF16), jnp.zeros((), BF16))
        for lvl in range(GLA_LEVELS):
            w = jnp.exp2(e_all[(1 + lvl) * c:(2 + lvl) * c]).astype(BF16)
            p = jnp.where(lv == lvl, _dot_nt(q * w, k * w).astype(BF16), p)
        o = o + _dot(p, v)

        st_ref[hd] = st * jnp.exp2(b2_last) + _dot_tn(v, k_up)

        y = o * _rms_scale(o) * ng_ref[:, vs]
        o_ref[:, vs] = (y * gz_ref[:, vs].astype(F32)).astype(o_ref.dtype)


def _gla(q, k, v, alpha, w2_cat, b_alpha, gz, norm_g, batch, seq):
    t = q.shape[0]
    c = GLA_CHUNK
    n_c = seq // c
    mats, lv = _gla_constants()
    tok = lambda bi, ni: (bi * n_c + ni, 0)
    return pl.pallas_call(
        _gla_kernel,
        grid=(batch, n_c),
        in_specs=[pl.BlockSpec((c, GLA_DK), tok), pl.BlockSpec((c, GLA_DK), tok),
                  pl.BlockSpec((c, GLA_DV), tok), pl.BlockSpec((c, RANK_PAD), tok),
                  _fixed((RANK_PAD, GLA_DK)), _fixed((1, GLA_DK)),
                  pl.BlockSpec((c, GLA_DV), tok), _fixed((1, GLA_DV)),
                  _fixed(mats.shape), _fixed(lv.shape)],
        out_specs=pl.BlockSpec((c, GLA_DV), tok),
        out_shape=jax.ShapeDtypeStruct((t, GLA_DV), BF16),
        scratch_shapes=[pltpu.VMEM((GLA_HEADS, GLA_DVH, GLA_DKH), F32)],
        compiler_params=_params(("arbitrary", "arbitrary")),
        name="gla",
    )(q, k, v, alpha, w2_cat, b_alpha.reshape(1, GLA_DK), gz, norm_g.reshape(1, GLA_DV),
      jnp.asarray(mats, BF16), jnp.asarray(lv, BF16))


def _mem_kv_kernel(m_ref, g_ref, w_ref, o_ref):
    m = m_ref[...]
    mn = (m * _rms_scale(m) * g_ref[...]).astype(BF16)
    o_ref[...] = _dot(mn, w_ref[...]).astype(o_ref.dtype)


def _mem_kv(mem2, g, w, batch, n_mem):
    d = mem2.shape[1]
    n = w.shape[1]
    return pl.pallas_call(
        _mem_kv_kernel,
        grid=(batch,),
        in_specs=[pl.BlockSpec((n_mem, d), lambda bi: (bi, 0)), _fixed((1, d)), _fixed((d, n))],
        out_specs=pl.BlockSpec((n_mem, n), lambda bi: (bi, 0)),
        out_shape=jax.ShapeDtypeStruct((batch * n_mem, n), BF16),
        compiler_params=_params(("parallel",)),
        name="mem_kv",
    )(mem2, g.reshape(1, d), w)


def _branches_kernel(c_ref, csz_ref, og_ref, mq_ref, msz_ref, kv_ref, g_ref,
                     lg_ref, lb_ref, wc_ref, bc_ref, wg_ref, wm_ref, y_ref):
    d = D_MODEL
    c = c_ref[...].astype(F32)
    dev = c - jnp.mean(c, axis=-1, keepdims=True)
    var = jnp.mean(dev * dev, axis=-1, keepdims=True)
    yh = dev * lax.rsqrt(var + EPS) * lg_ref[...] + lb_ref[...]
    u = _silu(yh) * csz_ref[...].astype(F32)
    y = (_dot(u.astype(BF16), wc_ref[...]) + bc_ref[...]) * g_ref[:, 0:d].astype(F32)

    y = y + _dot(og_ref[...], wg_ref[...]) * g_ref[:, d:2 * d].astype(F32)

    q = mq_ref[...]
    kv = kv_ref[...]
    outs = []
    for hd in range(MEM_HEADS):
        lo, hi = hd * MEM_HD, (hd + 1) * MEM_HD
        sc = _dot_nt(q[:, lo:hi], kv[:, lo:hi]) * (MEM_HD ** -0.5)
        e = jnp.exp(sc - jnp.max(sc, axis=-1, keepdims=True))
        p = e / jnp.sum(e, axis=-1, keepdims=True)
        outs.append(_dot(p.astype(BF16), kv[:, MEM_WIDTH + lo:MEM_WIDTH + hi]))
    om = jnp.concatenate(outs, axis=-1) * msz_ref[...].astype(F32)
    y = y + _dot(om.astype(BF16), wm_ref[...]) * g_ref[:, 2 * d:3 * d].astype(F32)
    y_ref[...] = y.astype(y_ref.dtype)


def _branches(conv, conv_sz, og, mq, mem_sz, mkv, gates, ln_g, ln_b, w_conv, b_conv, w_gla,
              w_mem, batch, seq, n_mem, tm):
    t, d = conv.shape
    n_t = seq // tm
    tok = lambda bi, ti: (bi * n_t + ti, 0)
    wide = lambda w: pl.BlockSpec((tm, w), tok)
    return pl.pallas_call(
        _branches_kernel,
        grid=(batch, n_t),
        in_specs=[wide(d), wide(d), wide(GLA_DV), wide(MEM_WIDTH), wide(MEM_WIDTH),
                  pl.BlockSpec((n_mem, 2 * MEM_WIDTH), lambda bi, ti: (bi, 0)),
                  wide(N_BRANCH * d),
                  _fixed((1, d)), _fixed((1, d)), _fixed((d, d)), _fixed((1, d)),
                  _fixed((GLA_DV, d)), _fixed((MEM_WIDTH, d))],
        out_specs=wide(d),
        out_shape=jax.ShapeDtypeStruct((t, d), BF16),
        compiler_params=_params(("parallel", "parallel")),
        name="branches",
    )(conv, conv_sz, og, mq, mem_sz, mkv, gates, ln_g.reshape(1, d), ln_b.reshape(1, d),
      w_conv, b_conv.reshape(1, d), w_gla, w_mem)


def _out_kernel(y_ref, x_ref, w_ref, g_ref, o_ref):
    r = x_ref[...] + _dot(y_ref[...], w_ref[...])
    o_ref[...] = r * _rms_scale(r) * g_ref[...]


def _out_proj(y, x, w, g, tm):
    t, d = x.shape
    row = lambda i: (i, 0)
    return pl.pallas_call(
        _out_kernel,
        grid=(t // tm,),
        in_specs=[pl.BlockSpec((tm, d), row), pl.BlockSpec((tm, d), row), _fixed((d, d)),
                  _fixed((1, d))],
        out_specs=pl.BlockSpec((tm, d), row),
        out_shape=jax.ShapeDtypeStruct((t, d), F32),
        compiler_params=_params(("parallel",)),
        name="out_proj",
    )(y, x, w, g.reshape(1, d))


def kernel(x, mem, ln_in_g, mem_ln_g, w_in, b_gate, dw_w, dw_b, conv_ln_g, conv_ln_b,
           w_conv_out, b_conv_out, w_alpha2, b_alpha, gla_norm_g, w_gla_out,
           w_mem_kv, w_mem_out, w_out, final_g):
    batch, seq, d = x.shape
    n_mem = mem.shape[1]
    depth = w_in.shape[0]
    assert depth == 1 and d == D_MODEL
    t = batch * seq
    xs = x.reshape(t, d)
    mem2 = mem.reshape(batch * n_mem, d)
    bf = lambda a: a.astype(BF16)
    l = 0

    widths = [d, d, d, GLA_DK, GLA_DK, GLA_DV, GLA_DV, GLA_RANK, MEM_WIDTH, MEM_WIDTH, N_BRANCH * d]
    offs = [int(o) for o in np.cumsum([0] + widths)]
    (o_ca, o_cb, o_cz, o_q, o_k, o_v, o_gz, o_al, o_mq, o_mz, o_g, o_end) = offs
    wt = w_in[l].T
    w2 = w_alpha2[l]
    w2_hi = bf(w2)
    w2_lo = bf(w2 - w2_hi.astype(F32))
    w2_cat = jnp.zeros((RANK_PAD, GLA_DK), BF16).at[:3 * GLA_RANK].set(
        jnp.concatenate([w2_hi, w2_hi, w2_lo], axis=0))

    w_qk, w_m = lax.optimization_barrier((wt[o_q:o_v], wt[o_al:o_g]))
    h, q, k, mq, mem_sz, alpha = _norm_qkm(xs, ln_in_g[l], bf(w_qk), bf(w_m), **TILES["norm_qkm"])

    u, conv_sz = _proj_call(_proj_conv_kernel, "proj_conv", h, wt, [o_ca, o_cb, o_cz], d,
                            [BF16, BF16], **TILES["proj_conv"])
    v, gla_sz = _proj_call(_proj_vz_kernel, "proj_vz", h, wt, [o_v, o_gz], GLA_DV,
                           [BF16, BF16], **TILES["proj_vz"])
    (gates,) = _proj_call(_proj_gate_kernel, "proj_gate", h, wt, [o_g], N_BRANCH * d, [BF16],
                          bias=b_gate[l], **TILES["proj_gate"])

    conv = _dwconv(u, dw_w[l], dw_b[l], batch, seq, **TILES["dwconv"])
    og = _gla(q, k, v, alpha, w2_cat, b_alpha[l], gla_sz, gla_norm_g[l], batch, seq)
    mkv = _mem_kv(mem2, mem_ln_g[l], bf(w_mem_kv[l]), batch, n_mem)

    y = _branches(conv, conv_sz, og, mq, mem_sz, mkv, gates, conv_ln_g[l], conv_ln_b[l],
                  bf(w_conv_out[l]), b_conv_out[l], bf(w_gla_out[l]), bf(w_mem_out[l]),
                  batch, seq, n_mem, **TILES["branches"])
    out = _out_proj(y, xs, bf(w_out[l]), final_g, **TILES["out_proj"])
    return out.reshape(batch, seq, d)
```

```python
import functools

import numpy as np
import jax
import jax.numpy as jnp
from jax import lax
from jax.experimental import pallas as pl
from jax.experimental.pallas import tpu as pltpu

F32 = jnp.float32
BF16 = jnp.bfloat16

D_MODEL = 2048
EPS = 1e-6
CONV_K = 31
GLA_HEADS = 4
GLA_DK = 1024
GLA_DV = 2048
GLA_DKH = GLA_DK // GLA_HEADS
GLA_DVH = GLA_DV // GLA_HEADS
GLA_RANK = 16
GLA_TAU = 16.0
MEM_HEADS = 4
MEM_HD = 128
MEM_WIDTH = MEM_HEADS * MEM_HD
N_BRANCH = 3

LANES = 128
VMEM_LIMIT = 60 * 1024 * 1024

GLA_CHUNK = 256
GLA_LEVELS = 8
RANK_PAD = LANES
CONV_HALO = 32
CONV_TB = 16
LOG2_E = 1.4426950408889634
W_ROW_ALIGN = 16

TILES = dict(
    norm_qkm=dict(tm=512),
    proj_conv=dict(tm=1024, tn=512, weight_buffers=2),
    proj_vz=dict(tm=1024, tn=1024, weight_buffers=1),
    proj_gate=dict(tm=2048, tn=1024, weight_buffers=2),
    dwconv=dict(ts=1024),
    branches=dict(tm=512),
    out_proj=dict(tm=1024),
)


def _params(sem):
    return pltpu.CompilerParams(dimension_semantics=sem, vmem_limit_bytes=VMEM_LIMIT)


def _fixed(shape, index=None):
    index = (0,) * len(shape) if index is None else tuple(index)
    return pl.BlockSpec(shape, lambda *_: index, pipeline_mode=pl.Buffered(1))


def _sigmoid(x):
    return 1.0 / (1.0 + jnp.exp(-x))


def _silu(x):
    return x * _sigmoid(x)


def _dot(a, b):
    return jnp.dot(a, b, preferred_element_type=F32)


def _dot_nt(a, b):
    return lax.dot_general(a, b, (((1,), (1,)), ((), ())), preferred_element_type=F32)


def _dot_tn(a, b):
    return lax.dot_general(a, b, (((0,), (0,)), ((), ())), preferred_element_type=F32)


def _rms_scale(x):
    return lax.rsqrt(jnp.mean(x * x, axis=-1, keepdims=True) + EPS)


def _round_weights_once(wt_refs, wb_ref):
    @pl.when(pl.program_id(1) == 0)
    def _():
        for n, wt_ref in enumerate(wt_refs):
            wb_ref[n] = wt_ref[...].T.astype(BF16)


def _proj_conv_kernel(h_ref, wa_ref, wb_ref, wz_ref, u_ref, sz_ref, w_ref):
    _round_weights_once((wa_ref, wb_ref, wz_ref), w_ref)
    h = h_ref[...]
    u_ref[...] = (_dot(h, w_ref[0]) * _sigmoid(_dot(h, w_ref[1]))).astype(BF16)
    sz_ref[...] = _silu(_dot(h, w_ref[2])).astype(BF16)


def _proj_vz_kernel(h_ref, wv_ref, wz_ref, v_ref, sz_ref, w_ref):
    _round_weights_once((wv_ref, wz_ref), w_ref)
    h = h_ref[...]
    v_ref[...] = _dot(h, w_ref[0]).astype(BF16)
    sz_ref[...] = _silu(_dot(h, w_ref[1])).astype(BF16)


def _proj_gate_kernel(h_ref, wg_ref, b_ref, g_ref, w_ref):
    _round_weights_once((wg_ref,), w_ref)
    g_ref[...] = _sigmoid(_dot(h_ref[...], w_ref[0]) + b_ref[...]).astype(BF16)


def _wt_rows(rows, d, index_map, buffers):
    def im(*idx):
        return pl.multiple_of(index_map(*idx), W_ROW_ALIGN), 0
    return pl.BlockSpec((pl.Element(rows), pl.Element(d)), im, pipeline_mode=pl.Buffered(buffers))


def _proj_call(kernel, name, h, wt, row_offsets, width, out_dtypes, tm, tn, bias=None,
               weight_buffers=2):
    t, d = h.shape
    assert all(off % W_ROW_ALIGN == 0 for off in row_offsets) and width % tn == 0
    in_specs = [pl.BlockSpec((tm, d), lambda j, i: (i, 0))]
    in_specs += [_wt_rows(tn, d, lambda j, i, o=off: o + j * tn, weight_buffers)
                 for off in row_offsets]
    args = [h] + [wt] * len(row_offsets)
    if bias is not None:
        in_specs.append(pl.BlockSpec((1, tn), lambda j, i: (0, j)))
        args.append(bias.reshape(1, width))
    return pl.pallas_call(
        kernel,
        grid=(width // tn, t // tm),
        in_specs=in_specs,
        out_specs=[pl.BlockSpec((tm, tn), lambda j, i: (i, j)) for _ in out_dtypes],
        out_shape=[jax.ShapeDtypeStruct((t, width), dt) for dt in out_dtypes],
        scratch_shapes=[pltpu.VMEM((len(row_offsets), d, tn), BF16)],
        compiler_params=_params(("arbitrary", "arbitrary")),
        name=name,
    )(*args)


def _norm_qkm_kernel(x_ref, g_ref, wqk_ref, wm_ref, h_ref, q_ref, k_ref, mq_ref, sz_ref, al_ref):
    x = x_ref[...]
    h = (x * _rms_scale(x) * g_ref[...]).astype(BF16)
    h_ref[...] = h
    al_ref[...] = _dot_nt(h, wm_ref[0:RANK_PAD])
    q_ref[...] = (_dot_nt(h, wqk_ref[0:GLA_DK]) * (GLA_DKH ** -0.5)).astype(BF16)
    k_ref[...] = _dot_nt(h, wqk_ref[GLA_DK:2 * GLA_DK]).astype(BF16)
    mq_ref[...] = _dot_nt(h, wm_ref[GLA_RANK:GLA_RANK + MEM_WIDTH]).astype(BF16)
    sz_ref[...] = _silu(_dot_nt(h, wm_ref[GLA_RANK + MEM_WIDTH:GLA_RANK + 2 * MEM_WIDTH])).astype(BF16)


def _norm_qkm(x, g, w_qk, w_m, tm):
    t, d = x.shape
    row = lambda i: (i, 0)
    out = lambda n: pl.BlockSpec((tm, n), row)
    return pl.pallas_call(
        _norm_qkm_kernel,
        grid=(t // tm,),
        in_specs=[pl.BlockSpec((tm, d), row), _fixed((1, d)), _fixed(w_qk.shape), _fixed(w_m.shape)],
        out_specs=[out(d), out(GLA_DK), out(GLA_DK), out(MEM_WIDTH), out(MEM_WIDTH), out(RANK_PAD)],
        out_shape=[jax.ShapeDtypeStruct((t, d), BF16),
                   jax.ShapeDtypeStruct((t, GLA_DK), BF16), jax.ShapeDtypeStruct((t, GLA_DK), BF16),
                   jax.ShapeDtypeStruct((t, MEM_WIDTH), BF16),
                   jax.ShapeDtypeStruct((t, MEM_WIDTH), BF16),
                   jax.ShapeDtypeStruct((t, RANK_PAD), F32)],
        compiler_params=_params(("parallel",)),
        name="norm_qkm",
    )(x, g.reshape(1, d), w_qk, w_m)


def _dwconv_kernel(u_ref, w_ref, b_ref, o_ref, ext_ref, acc_ref, *, ts):
    s = pl.program_id(1)
    sub = ext_ref.shape[1]

    @pl.when(s == 0)
    def _():
        ext_ref[0:CONV_HALO] = jnp.zeros((CONV_HALO,) + ext_ref.shape[1:], F32)

    @pl.when(s > 0)
    def _():
        ext_ref[0:CONV_HALO] = ext_ref[ts:ts + CONV_HALO]

    ext_ref[CONV_HALO:CONV_HALO + ts] = pltpu.einshape(
        "t(sl)->tsl", u_ref[...].astype(F32), s=sub)
    lead = CONV_HALO - (CONV_K - 1)

    def body(i, carry):
        t0 = pl.multiple_of(i * CONV_TB, CONV_TB)
        acc = jnp.broadcast_to(b_ref[...], (CONV_TB,) + ext_ref.shape[1:])
        for j in range(CONV_K):
            acc = acc + w_ref[j] * ext_ref[pl.ds(t0 + lead + j, CONV_TB)]
        acc_ref[pl.ds(t0, CONV_TB)] = acc
        return carry

    lax.fori_loop(0, ts // CONV_TB, body, 0)
    o_ref[...] = pltpu.einshape("tsl->t(sl)", acc_ref[...]).astype(o_ref.dtype)


def _dwconv(u, w, b, batch, seq, ts):
    t, c = u.shape
    sub = c // LANES
    n_s = seq // ts
    tok = lambda bi, si: (bi * n_s + si, 0)
    return pl.pallas_call(
        functools.partial(_dwconv_kernel, ts=ts),
        grid=(batch, n_s),
        in_specs=[pl.BlockSpec((ts, c), tok), _fixed((CONV_K, sub, LANES)), _fixed((1, sub, LANES))],
        out_specs=pl.BlockSpec((ts, c), tok),
        out_shape=jax.ShapeDtypeStruct((t, c), BF16),
        scratch_shapes=[pltpu.VMEM((ts + CONV_HALO, sub, LANES), F32),
                        pltpu.VMEM((ts, sub, LANES), F32)],
        compiler_params=_params(("arbitrary", "arbitrary")),
        name="dwconv",
    )(u, w.reshape(CONV_K, sub, LANES), b.reshape(1, sub, LANES))


def _gla_constants():
    c = GLA_CHUNK
    t = np.arange(c)[:, None]
    u = np.arange(c)[None, :]
    mats = np.zeros((GLA_LEVELS + 1, c, c), np.float32)
    mats[0] = u <= t
    for lvl in range(GLA_LEVELS):
        m = 1 << lvl
        pos = t % (2 * m)
        r = t - pos + m - 1
        mats[1 + lvl] = ((pos >= m) & (u > r) & (u <= t)) | ((pos < m) & (u > t) & (u <= r))
    x = t ^ u
    lv = np.where(x > 0, np.floor(np.log2(np.maximum(x, 1))).astype(np.int32), GLA_LEVELS)
    lv = np.where(u > t, -1, lv).astype(np.int32)
    return mats.reshape((GLA_LEVELS + 1) * c, c), lv


def _gla_kernel(q_ref, k_ref, v_ref, al_ref, w2_ref, ba_ref, gz_ref, ng_ref, mats_ref, lv_ref,
                o_ref, st_ref):
    c = GLA_CHUNK

    @pl.when(pl.program_id(1) == 0)
    def _():
        st_ref[...] = jnp.zeros_like(st_ref)

    al = al_ref[...]
    lane = lax.broadcasted_iota(jnp.int32, al.shape, 1)
    gate = jnp.where(lane < GLA_RANK, al, 0.0)
    hi = gate.astype(BF16).astype(F32)
    lo = gate - hi
    lhs = (hi + pltpu.roll(lo, GLA_RANK, 1) + pltpu.roll(hi, 2 * GLA_RANK, 1)).astype(BF16)
    x = _dot(lhs, w2_ref[...]) + ba_ref[...]
    log_a = (jnp.minimum(x, 0.0) - jnp.log1p(jnp.exp(-jnp.abs(x)))) * (1.0 / GLA_TAU)
    la = (log_a * LOG2_E).astype(BF16)

    lv = lv_ref[...]
    for hd in range(GLA_HEADS):
        ks = slice(hd * GLA_DKH, (hd + 1) * GLA_DKH)
        vs = slice(hd * GLA_DVH, (hd + 1) * GLA_DVH)
        q = q_ref[:, ks]
        k = k_ref[:, ks]
        v = v_ref[:, vs]

        e_all = _dot(mats_ref[...], la[:, ks])
        b2 = e_all[:c]
        b2_last = b2[c - 1:c, :]
        q_in = q * jnp.exp2(b2).astype(BF16)
        k_up = k * jnp.exp2(b2_last - b2).astype(BF16)

        st = st_ref[hd]
        o = _dot_nt(q_in, st.astype(BF16))

        p = jnp.where(lv == GLA_LEVELS, _dot_nt(q, k).astype(BF16), jnp.zeros((), BF16))
        for lvl in range(GLA_LEVELS):
            w = jnp.exp2(e_all[(1 + lvl) * c:(2 + lvl) * c]).astype(BF16)
            p = jnp.where(lv == lvl, _dot_nt(q * w, k * w).astype(BF16), p)
        o = o + _dot(p, v)

        st_ref[hd] = st * jnp.exp2(b2_last) + _dot_tn(v, k_up)

        y = o * _rms_scale(o) * ng_ref[:, vs]
        o_ref[:, vs] = (y * gz_ref[:, vs].astype(F32)).astype(o_ref.dtype)


def _gla(q, k, v, alpha, w2_cat, b_alpha, gz, norm_g, batch, seq):
    t = q.shape[0]
    c = GLA_CHUNK
    n_c = seq // c
    mats, lv = _gla_constants()
    tok = lambda bi, ni: (bi * n_c + ni, 0)
    return pl.pallas_call(
        _gla_kernel,
        grid=(batch, n_c),
        in_specs=[pl.BlockSpec((c, GLA_DK), tok), pl.BlockSpec((c, GLA_DK), tok),
                  pl.BlockSpec((c, GLA_DV), tok), pl.BlockSpec((c, RANK_PAD), tok),
                  _fixed((RANK_PAD, GLA_DK)), _fixed((1, GLA_DK)),
                  pl.BlockSpec((c, GLA_DV), tok), _fixed((1, GLA_DV)),
                  _fixed(mats.shape), _fixed(lv.shape)],
        out_specs=pl.BlockSpec((c, GLA_DV), tok),
        out_shape=jax.ShapeDtypeStruct((t, GLA_DV), BF16),
        scratch_shapes=[pltpu.VMEM((GLA_HEADS, GLA_DVH, GLA_DKH), F32)],
        compiler_params=_params(("arbitrary", "arbitrary")),
        name="gla",
    )(q, k, v, alpha, w2_cat, b_alpha.reshape(1, GLA_DK), gz, norm_g.reshape(1, GLA_DV),
      jnp.asarray(mats, BF16), jnp.asarray(lv, BF16))


def _mem_kv_kernel(m_ref, g_ref, w_ref, o_ref):
    m = m_ref[...]
    mn = (m * _rms_scale(m) * g_ref[...]).astype(BF16)
    o_ref[...] = _dot(mn, w_ref[...]).astype(o_ref.dtype)


def _mem_kv(mem2, g, w, batch, n_mem):
    d = mem2.shape[1]
    n = w.shape[1]
    return pl.pallas_call(
        _mem_kv_kernel,
        grid=(batch,),
        in_specs=[pl.BlockSpec((n_mem, d), lambda bi: (bi, 0)), _fixed((1, d)), _fixed((d, n))],
        out_specs=pl.BlockSpec((n_mem, n), lambda bi: (bi, 0)),
        out_shape=jax.ShapeDtypeStruct((batch * n_mem, n), BF16),
        compiler_params=_params(("parallel",)),
        name="mem_kv",
    )(mem2, g.reshape(1, d), w)


def _branches_kernel(c_ref, csz_ref, og_ref, mq_ref, msz_ref, kv_ref, g_ref,
                     lg_ref, lb_ref, wc_ref, bc_ref, wg_ref, wm_ref, y_ref):
    d = D_MODEL
    c = c_ref[...].astype(F32)
    dev = c - jnp.mean(c, axis=-1, keepdims=True)
    var = jnp.mean(dev * dev, axis=-1, keepdims=True)
    yh = dev * lax.rsqrt(var + EPS) * lg_ref[...] + lb_ref[...]
    u = _silu(yh) * csz_ref[...].astype(F32)
    y = (_dot(u.astype(BF16), wc_ref[...]) + bc_ref[...]) * g_ref[:, 0:d].astype(F32)

    y = y + _dot(og_ref[...], wg_ref[...]) * g_ref[:, d:2 * d].astype(F32)

    q = mq_ref[...]
    kv = kv_ref[...]
    outs = []
    for hd in range(MEM_HEADS):
        lo, hi = hd * MEM_HD, (hd + 1) * MEM_HD
        sc = _dot_nt(q[:, lo:hi], kv[:, lo:hi]) * (MEM_HD ** -0.5)
        e = jnp.exp(sc - jnp.max(sc, axis=-1, keepdims=True))
        p = e / jnp.sum(e, axis=-1, keepdims=True)
        outs.append(_dot(p.astype(BF16), kv[:, MEM_WIDTH + lo:MEM_WIDTH + hi]))
    om = jnp.concatenate(outs, axis=-1) * msz_ref[...].astype(F32)
    y = y + _dot(om.astype(BF16), wm_ref[...]) * g_ref[:, 2 * d:3 * d].astype(F32)
    y_ref[...] = y.astype(y_ref.dtype)


def _branches(conv, conv_sz, og, mq, mem_sz, mkv, gates, ln_g, ln_b, w_conv, b_conv, w_gla,
              w_mem, batch, seq, n_mem, tm):
    t, d = conv.shape
    n_t = seq // tm
    tok = lambda bi, ti: (bi * n_t + ti, 0)
    wide = lambda w: pl.BlockSpec((tm, w), tok)
    return pl.pallas_call(
        _branches_kernel,
        grid=(batch, n_t),
        in_specs=[wide(d), wide(d), wide(GLA_DV), wide(MEM_WIDTH), wide(MEM_WIDTH),
                  pl.BlockSpec((n_mem, 2 * MEM_WIDTH), lambda bi, ti: (bi, 0)),
                  wide(N_BRANCH * d),
                  _fixed((1, d)), _fixed((1, d)), _fixed((d, d)), _fixed((1, d)),
                  _fixed((GLA_DV, d)), _fixed((MEM_WIDTH, d))],
        out_specs=wide(d),
        out_shape=jax.ShapeDtypeStruct((t, d), BF16),
        compiler_params=_params(("parallel", "parallel")),
        name="branches",
    )(conv, conv_sz, og, mq, mem_sz, mkv, gates, ln_g.reshape(1, d), ln_b.reshape(1, d),
      w_conv, b_conv.reshape(1, d), w_gla, w_mem)


def _out_kernel(y_ref, x_ref, w_ref, g_ref, o_ref):
    r = x_ref[...] + _dot(y_ref[...], w_ref[...])
    o_ref[...] = r * _rms_scale(r) * g_ref[...]


def _out_proj(y, x, w, g, tm):
    t, d = x.shape
    row = lambda i: (i, 0)
    return pl.pallas_call(
        _out_kernel,
        grid=(t // tm,),
        in_specs=[pl.BlockSpec((tm, d), row), pl.BlockSpec((tm, d), row), _fixed((d, d)),
                  _fixed((1, d))],
        out_specs=pl.BlockSpec((tm, d), row),
        out_shape=jax.ShapeDtypeStruct((t, d), F32),
        compiler_params=_params(("parallel",)),
        name="out_proj",
    )(y, x, w, g.reshape(1, d))


def kernel(x, mem, ln_in_g, mem_ln_g, w_in, b_gate, dw_w, dw_b, conv_ln_g, conv_ln_b,
           w_conv_out, b_conv_out, w_alpha2, b_alpha, gla_norm_g, w_gla_out,
           w_mem_kv, w_mem_out, w_out, final_g):
    batch, seq, d = x.shape
    n_mem = mem.shape[1]
    depth = w_in.shape[0]
    assert depth == 1 and d == D_MODEL
    t = batch * seq
    xs = x.reshape(t, d)
    mem2 = mem.reshape(batch * n_mem, d)
    bf = lambda a: a.astype(BF16)
    l = 0

    widths = [d, d, d, GLA_DK, GLA_DK, GLA_DV, GLA_DV, GLA_RANK, MEM_WIDTH, MEM_WIDTH, N_BRANCH * d]
    offs = [int(o) for o in np.cumsum([0] + widths)]
    (o_ca, o_cb, o_cz, o_q, o_k, o_v, o_gz, o_al, o_mq, o_mz, o_g, o_end) = offs
    wt = w_in[l].T
    w2 = w_alpha2[l]
    w2_hi = bf(w2)
    w2_lo = bf(w2 - w2_hi.astype(F32))
    w2_cat = jnp.zeros((RANK_PAD, GLA_DK), BF16).at[:3 * GLA_RANK].set(
        jnp.concatenate([w2_hi, w2_hi, w2_lo], axis=0))

    w_qk, w_m = lax.optimization_barrier((wt[o_q:o_v], wt[o_al:o_g]))
    h, q, k, mq, mem_sz, alpha = _norm_qkm(xs, ln_in_g[l], bf(w_qk), bf(w_m), **TILES["norm_qkm"])

    u, conv_sz = _proj_call(_proj_conv_kernel, "proj_conv", h, wt, [o_ca, o_cb, o_cz], d,
                            [BF16, BF16], **TILES["proj_conv"])
    v, gla_sz = _proj_call(_proj_vz_kernel, "proj_vz", h, wt, [o_v, o_gz], GLA_DV,
                           [BF16, BF16], **TILES["proj_vz"])
    (gates,) = _proj_call(_proj_gate_kernel, "proj_gate", h, wt, [o_g], N_BRANCH * d, [BF16],
                          bias=b_gate[l], **TILES["proj_gate"])

    conv = _dwconv(u, dw_w[l], dw_b[l], batch, seq, **TILES["dwconv"])
    og = _gla(q, k, v, alpha, w2_cat, b_alpha[l], gla_sz, gla_norm_g[l], batch, seq)
    mkv = _mem_kv(mem2, mem_ln_g[l], bf(w_mem_kv[l]), batch, n_mem)

    y = _branches(conv, conv_sz, og, mq, mem_sz, mkv, gates, conv_ln_g[l], conv_ln_b[l],
                  bf(w_conv_out[l]), b_conv_out[l], bf(w_gla_out[l]), bf(w_mem_out[l]),
                  batch, seq, n_mem, **TILES["branches"])
    out = _out_proj(y, xs, bf(w_out[l]), final_g, **TILES["out_proj"])
    return out.reshape(batch, seq, d)
```

```python
import functools

import numpy as np
import jax
import jax.numpy as jnp
from jax import lax
from jax.experimental import pallas as pl
from jax.experimental.pallas import tpu as pltpu

F32 = jnp.float32
BF16 = jnp.bfloat16

D_MODEL = 2048
EPS = 1e-6
CONV_K = 31
GLA_HEADS = 4
GLA_DK = 1024
GLA_DV = 2048
GLA_DKH = GLA_DK // GLA_HEADS
GLA_DVH = GLA_DV // GLA_HEADS
GLA_RANK = 16
GLA_TAU = 16.0
MEM_HEADS = 4
MEM_HD = 128
MEM_WIDTH = MEM_HEADS * MEM_HD
N_BRANCH = 3

LANES = 128
VMEM_LIMIT = 60 * 1024 * 1024

GLA_CHUNK = 256
GLA_LEVELS = 8
GLA_CHUNKS_PER_STEP = 2
RANK_PAD = LANES
CONV_HALO = 32
CONV_TB = 16
LOG2_E = 1.4426950408889634
W_ROW_ALIGN = 16

TILES = dict(
    norm_qkm=dict(tm=512),
    proj_conv=dict(tm=1024, tn=512, weight_buffers=2),
    proj_vz=dict(tm=1024, tn=1024, weight_buffers=1),
    proj_gate=dict(tm=2048, tn=1024, weight_buffers=2),
    dwconv=dict(ts=1024),
    branches=dict(tm=512),
    out_proj=dict(tm=1024),
)


def _params(sem):
    return pltpu.CompilerParams(dimension_semantics=sem, vmem_limit_bytes=VMEM_LIMIT)


def _fixed(shape, index=None):
    index = (0,) * len(shape) if index is None else tuple(index)
    return pl.BlockSpec(shape, lambda *_: index, pipeline_mode=pl.Buffered(1))


def _sigmoid(x):
    return 1.0 / (1.0 + jnp.exp(-x))


def _silu(x):
    return x * _sigmoid(x)


def _dot(a, b):
    return jnp.dot(a, b, preferred_element_type=F32)


def _dot_nt(a, b):
    return lax.dot_general(a, b, (((1,), (1,)), ((), ())), preferred_element_type=F32)


def _dot_tn(a, b):
    return lax.dot_general(a, b, (((0,), (0,)), ((), ())), preferred_element_type=F32)


def _rms_scale(x):
    return lax.rsqrt(jnp.mean(x * x, axis=-1, keepdims=True) + EPS)


def _round_weights_once(wt_refs, wb_ref):
    @pl.when(pl.program_id(1) == 0)
    def _():
        for n, wt_ref in enumerate(wt_refs):
            wb_ref[n] = wt_ref[...].T.astype(BF16)


def _proj_conv_kernel(h_ref, wa_ref, wb_ref, wz_ref, u_ref, sz_ref, w_ref):
    _round_weights_once((wa_ref, wb_ref, wz_ref), w_ref)
    h = h_ref[...]
    u_ref[...] = (_dot(h, w_ref[0]) * _sigmoid(_dot(h, w_ref[1]))).astype(BF16)
    sz_ref[...] = _silu(_dot(h, w_ref[2])).astype(BF16)


def _proj_vz_kernel(h_ref, wv_ref, wz_ref, v_ref, sz_ref, w_ref):
    _round_weights_once((wv_ref, wz_ref), w_ref)
    h = h_ref[...]
    v_ref[...] = _dot(h, w_ref[0]).astype(BF16)
    sz_ref[...] = _silu(_dot(h, w_ref[1])).astype(BF16)


def _proj_gate_kernel(h_ref, wg_ref, b_ref, g_ref, w_ref):
    _round_weights_once((wg_ref,), w_ref)
    g_ref[...] = _sigmoid(_dot(h_ref[...], w_ref[0]) + b_ref[...]).astype(BF16)


def _wt_rows(rows, d, index_map, buffers):
    def im(*idx):
        return pl.multiple_of(index_map(*idx), W_ROW_ALIGN), 0
    return pl.BlockSpec((pl.Element(rows), pl.Element(d)), im, pipeline_mode=pl.Buffered(buffers))


def _proj_call(kernel, name, h, wt, row_offsets, width, out_dtypes, tm, tn, bias=None,
               weight_buffers=2):
    t, d = h.shape
    assert all(off % W_ROW_ALIGN == 0 for off in row_offsets) and width % tn == 0
    in_specs = [pl.BlockSpec((tm, d), lambda j, i: (i, 0))]
    in_specs += [_wt_rows(tn, d, lambda j, i, o=off: o + j * tn, weight_buffers)
                 for off in row_offsets]
    args = [h] + [wt] * len(row_offsets)
    if bias is not None:
        in_specs.append(pl.BlockSpec((1, tn), lambda j, i: (0, j)))
        args.append(bias.reshape(1, width))
    return pl.pallas_call(
        kernel,
        grid=(width // tn, t // tm),
        in_specs=in_specs,
        out_specs=[pl.BlockSpec((tm, tn), lambda j, i: (i, j)) for _ in out_dtypes],
        out_shape=[jax.ShapeDtypeStruct((t, width), dt) for dt in out_dtypes],
        scratch_shapes=[pltpu.VMEM((len(row_offsets), d, tn), BF16)],
        compiler_params=_params(("arbitrary", "arbitrary")),
        name=name,
    )(*args)


def _norm_qkm_kernel(x_ref, g_ref, wqk_ref, wm_ref, h_ref, q_ref, k_ref, mq_ref, sz_ref, al_ref):
    x = x_ref[...]
    h = (x * _rms_scale(x) * g_ref[...]).astype(BF16)
    h_ref[...] = h
    al_ref[...] = _dot_nt(h, wm_ref[0:RANK_PAD])
    q_ref[...] = (_dot_nt(h, wqk_ref[0:GLA_DK]) * (GLA_DKH ** -0.5)).astype(BF16)
    k_ref[...] = _dot_nt(h, wqk_ref[GLA_DK:2 * GLA_DK]).astype(BF16)
    mq_ref[...] = _dot_nt(h, wm_ref[GLA_RANK:GLA_RANK + MEM_WIDTH]).astype(BF16)
    sz_ref[...] = _silu(_dot_nt(h, wm_ref[GLA_RANK + MEM_WIDTH:GLA_RANK + 2 * MEM_WIDTH])).astype(BF16)


def _norm_qkm(x, g, w_qk, w_m, tm):
    t, d = x.shape
    row = lambda i: (i, 0)
    out = lambda n: pl.BlockSpec((tm, n), row)
    return pl.pallas_call(
        _norm_qkm_kernel,
        grid=(t // tm,),
        in_specs=[pl.BlockSpec((tm, d), row), _fixed((1, d)), _fixed(w_qk.shape), _fixed(w_m.shape)],
        out_specs=[out(d), out(GLA_DK), out(GLA_DK), out(MEM_WIDTH), out(MEM_WIDTH), out(RANK_PAD)],
        out_shape=[jax.ShapeDtypeStruct((t, d), BF16),
                   jax.ShapeDtypeStruct((t, GLA_DK), BF16), jax.ShapeDtypeStruct((t, GLA_DK), BF16),
                   jax.ShapeDtypeStruct((t, MEM_WIDTH), BF16),
                   jax.ShapeDtypeStruct((t, MEM_WIDTH), BF16),
                   jax.ShapeDtypeStruct((t, RANK_PAD), F32)],
        compiler_params=_params(("parallel",)),
        name="norm_qkm",
    )(x, g.reshape(1, d), w_qk, w_m)


def _dwconv_kernel(u_ref, w_ref, b_ref, o_ref, ext_ref, acc_ref, *, ts):
    s = pl.program_id(1)
    sub = ext_ref.shape[1]

    @pl.when(s == 0)
    def _():
        ext_ref[0:CONV_HALO] = jnp.zeros((CONV_HALO,) + ext_ref.shape[1:], F32)

    @pl.when(s > 0)
    def _():
        ext_ref[0:CONV_HALO] = ext_ref[ts:ts + CONV_HALO]

    ext_ref[CONV_HALO:CONV_HALO + ts] = pltpu.einshape(
        "t(sl)->tsl", u_ref[...].astype(F32), s=sub)
    lead = CONV_HALO - (CONV_K - 1)

    def body(i, carry):
        t0 = pl.multiple_of(i * CONV_TB, CONV_TB)
        acc = jnp.broadcast_to(b_ref[...], (CONV_TB,) + ext_ref.shape[1:])
        for j in range(CONV_K):
            acc = acc + w_ref[j] * ext_ref[pl.ds(t0 + lead + j, CONV_TB)]
        acc_ref[pl.ds(t0, CONV_TB)] = acc
        return carry

    lax.fori_loop(0, ts // CONV_TB, body, 0)
    o_ref[...] = pltpu.einshape("tsl->t(sl)", acc_ref[...]).astype(o_ref.dtype)


def _dwconv(u, w, b, batch, seq, ts):
    t, c = u.shape
    sub = c // LANES
    n_s = seq // ts
    tok = lambda bi, si: (bi * n_s + si, 0)
    return pl.pallas_call(
        functools.partial(_dwconv_kernel, ts=ts),
        grid=(batch, n_s),
        in_specs=[pl.BlockSpec((ts, c), tok), _fixed((CONV_K, sub, LANES)), _fixed((1, sub, LANES))],
        out_specs=pl.BlockSpec((ts, c), tok),
        out_shape=jax.ShapeDtypeStruct((t, c), BF16),
        scratch_shapes=[pltpu.VMEM((ts + CONV_HALO, sub, LANES), F32),
                        pltpu.VMEM((ts, sub, LANES), F32)],
        compiler_params=_params(("arbitrary", "arbitrary")),
        name="dwconv",
    )(u, w.reshape(CONV_K, sub, LANES), b.reshape(1, sub, LANES))


def _gla_constants():
    c = GLA_CHUNK
    t = np.arange(c)[:, None]
    u = np.arange(c)[None, :]
    mats = np.zeros((GLA_LEVELS + 1, c, c), np.float32)
    mats[0] = u <= t
    for lvl in range(GLA_LEVELS):
        m = 1 << lvl
        pos = t % (2 * m)
        r = t - pos + m - 1
        mats[1 + lvl] = ((pos >= m) & (u > r) & (u <= t)) | ((pos < m) & (u > t) & (u <= r))
    x = t ^ u
    lv = np.where(x > 0, np.floor(np.log2(np.maximum(x, 1))).astype(np.int32), GLA_LEVELS)
    lv = np.where(u > t, -1, lv).astype(np.int32)
    return mats.reshape((GLA_LEVELS + 1) * c, c), lv


def _gla_kernel(q_ref, k_ref, v_ref, al_ref, w2_ref, ba_ref, gz_ref, ng_ref, mats_ref, lv_ref,
                o_ref, st_ref):
    c = GLA_CHUNK

    @pl.when(pl.program_id(1) == 0)
    def _():
        st_ref[...] = jnp.zeros_like(st_ref)

    al = al_ref[...]
    lane = lax.broadcasted_iota(jnp.int32, al.shape, 1)
    gate = jnp.where(lane < GLA_RANK, al, 0.0)
    hi = gate.astype(BF16).astype(F32)
    lo = gate - hi
    lhs = (hi + pltpu.roll(lo, GLA_RANK, 1) + pltpu.roll(hi, 2 * GLA_RANK, 1)).astype(BF16)
    x = _dot(lhs, w2_ref[...]) + ba_ref[...]
    log_a = (jnp.minimum(x, 0.0) - jnp.log1p(jnp.exp(-jnp.abs(x)))) * (1.0 / GLA_TAU)
    la = (log_a * LOG2_E).astype(BF16)

    lv = lv_ref[...]
    for hd in range(GLA_HEADS):
        ks = slice(hd * GLA_DKH, (hd + 1) * GLA_DKH)
        vs = slice(hd * GLA_DVH, (hd + 1) * GLA_DVH)
        st = st_ref[hd]
        for sc in range(q_ref.shape[0] // c):
            rows = slice(sc * c, (sc + 1) * c)
            q = q_ref[rows, ks]
            k = k_ref[rows, ks]
            v = v_ref[rows, vs]

            e_all = _dot(mats_ref[...], la[rows, ks])
            b2 = e_all[:c]
            b2_last = b2[c - 1:c, :]
            q_in = q * jnp.exp2(b2).astype(BF16)
            k_up = k * jnp.exp2(b2_last - b2).astype(BF16)

            o = _dot_nt(q_in, st.astype(BF16))

            p = jnp.where(lv == GLA_LEVELS, _dot_nt(q, k).astype(BF16), jnp.zeros((), BF16))
            for lvl in range(GLA_LEVELS):
                w = jnp.exp2(e_all[(1 + lvl) * c:(2 + lvl) * c]).astype(BF16)
                p = jnp.where(lv == lvl, _dot_nt(q * w, k * w).astype(BF16), p)
            o = o + _dot(p, v)

            st = st * jnp.exp2(b2_last) + _dot_tn(v, k_up)

            y = o * _rms_scale(o) * ng_ref[:, vs]
            o_ref[rows, vs] = (y * gz_ref[rows, vs].astype(F32)).astype(o_ref.dtype)
        st_ref[hd] = st


def _gla(q, k, v, alpha, w2_cat, b_alpha, gz, norm_g, batch, seq):
    t = q.shape[0]
    c = GLA_CHUNK * GLA_CHUNKS_PER_STEP
    n_c = seq // c
    mats, lv = _gla_constants()
    tok = lambda bi, ni: (bi * n_c + ni, 0)
    return pl.pallas_call(
        _gla_kernel,
        grid=(batch, n_c),
        in_specs=[pl.BlockSpec((c, GLA_DK), tok), pl.BlockSpec((c, GLA_DK), tok),
                  pl.BlockSpec((c, GLA_DV), tok), pl.BlockSpec((c, RANK_PAD), tok),
                  _fixed((RANK_PAD, GLA_DK)), _fixed((1, GLA_DK)),
                  pl.BlockSpec((c, GLA_DV), tok), _fixed((1, GLA_DV)),
                  _fixed(mats.shape), _fixed(lv.shape)],
        out_specs=pl.BlockSpec((c, GLA_DV), tok),
        out_shape=jax.ShapeDtypeStruct((t, GLA_DV), BF16),
        scratch_shapes=[pltpu.VMEM((GLA_HEADS, GLA_DVH, GLA_DKH), F32)],
        compiler_params=_params(("arbitrary", "arbitrary")),
        name="gla",
    )(q, k, v, alpha, w2_cat, b_alpha.reshape(1, GLA_DK), gz, norm_g.reshape(1, GLA_DV),
      jnp.asarray(mats, BF16), jnp.asarray(lv, BF16))


def _mem_kv_kernel(m_ref, g_ref, w_ref, o_ref):
    m = m_ref[...]
    mn = (m * _rms_scale(m) * g_ref[...]).astype(BF16)
    o_ref[...] = _dot(mn, w_ref[...]).astype(o_ref.dtype)


def _mem_kv(mem2, g, w, batch, n_mem):
    d = mem2.shape[1]
    n = w.shape[1]
    return pl.pallas_call(
        _mem_kv_kernel,
        grid=(batch,),
        in_specs=[pl.BlockSpec((n_mem, d), lambda bi: (bi, 0)), _fixed((1, d)), _fixed((d, n))],
        out_specs=pl.BlockSpec((n_mem, n), lambda bi: (bi, 0)),
        out_shape=jax.ShapeDtypeStruct((batch * n_mem, n), BF16),
        compiler_params=_params(("parallel",)),
        name="mem_kv",
    )(mem2, g.reshape(1, d), w)


def _branches_kernel(c_ref, csz_ref, og_ref, mq_ref, msz_ref, kv_ref, g_ref,
                     lg_ref, lb_ref, wc_ref, bc_ref, wg_ref, wm_ref, y_ref):
    d = D_MODEL
    c = c_ref[...].astype(F32)
    dev = c - jnp.mean(c, axis=-1, keepdims=True)
    var = jnp.mean(dev * dev, axis=-1, keepdims=True)
    yh = dev * lax.rsqrt(var + EPS) * lg_ref[...] + lb_ref[...]
    u = _silu(yh) * csz_ref[...].astype(F32)
    y = (_dot(u.astype(BF16), wc_ref[...]) + bc_ref[...]) * g_ref[:, 0:d].astype(F32)

    y = y + _dot(og_ref[...], wg_ref[...]) * g_ref[:, d:2 * d].astype(F32)

    q = mq_ref[...]
    kv = kv_ref[...]
    outs = []
    for hd in range(MEM_HEADS):
        lo, hi = hd * MEM_HD, (hd + 1) * MEM_HD
        sc = _dot_nt(q[:, lo:hi], kv[:, lo:hi]) * (MEM_HD ** -0.5)
        e = jnp.exp(sc - jnp.max(sc, axis=-1, keepdims=True))
        p = e / jnp.sum(e, axis=-1, keepdims=True)
        outs.append(_dot(p.astype(BF16), kv[:, MEM_WIDTH + lo:MEM_WIDTH + hi]))
    om = jnp.concatenate(outs, axis=-1) * msz_ref[...].astype(F32)
    y = y + _dot(om.astype(BF16), wm_ref[...]) * g_ref[:, 2 * d:3 * d].astype(F32)
    y_ref[...] = y.astype(y_ref.dtype)


def _branches(conv, conv_sz, og, mq, mem_sz, mkv, gates, ln_g, ln_b, w_conv, b_conv, w_gla,
              w_mem, batch, seq, n_mem, tm):
    t, d = conv.shape
    n_t = seq // tm
    tok = lambda bi, ti: (bi * n_t + ti, 0)
    wide = lambda w: pl.BlockSpec((tm, w), tok)
    return pl.pallas_call(
        _branches_kernel,
        grid=(batch, n_t),
        in_specs=[wide(d), wide(d), wide(GLA_DV), wide(MEM_WIDTH), wide(MEM_WIDTH),
                  pl.BlockSpec((n_mem, 2 * MEM_WIDTH), lambda bi, ti: (bi, 0)),
                  wide(N_BRANCH * d),
                  _fixed((1, d)), _fixed((1, d)), _fixed((d, d)), _fixed((1, d)),
                  _fixed((GLA_DV, d)), _fixed((MEM_WIDTH, d))],
        out_specs=wide(d),
        out_shape=jax.ShapeDtypeStruct((t, d), BF16),
        compiler_params=_params(("parallel", "parallel")),
        name="branches",
    )(conv, conv_sz, og, mq, mem_sz, mkv, gates, ln_g.reshape(1, d), ln_b.reshape(1, d),
      w_conv, b_conv.reshape(1, d), w_gla, w_mem)


def _out_kernel(y_ref, x_ref, w_ref, g_ref, o_ref):
    r = x_ref[...] + _dot(y_ref[...], w_ref[...])
    o_ref[...] = r * _rms_scale(r) * g_ref[...]


def _out_proj(y, x, w, g, tm):
    t, d = x.shape
    row = lambda i: (i, 0)
    return pl.pallas_call(
        _out_kernel,
        grid=(t // tm,),
        in_specs=[pl.BlockSpec((tm, d), row), pl.BlockSpec((tm, d), row), _fixed((d, d)),
                  _fixed((1, d))],
        out_specs=pl.BlockSpec((tm, d), row),
        out_shape=jax.ShapeDtypeStruct((t, d), F32),
        compiler_params=_params(("parallel",)),
        name="out_proj",
    )(y, x, w, g.reshape(1, d))


def kernel(x, mem, ln_in_g, mem_ln_g, w_in, b_gate, dw_w, dw_b, conv_ln_g, conv_ln_b,
           w_conv_out, b_conv_out, w_alpha2, b_alpha, gla_norm_g, w_gla_out,
           w_mem_kv, w_mem_out, w_out, final_g):
    batch, seq, d = x.shape
    n_mem = mem.shape[1]
    depth = w_in.shape[0]
    assert depth == 1 and d == D_MODEL
    t = batch * seq
    xs = x.reshape(t, d)
    mem2 = mem.reshape(batch * n_mem, d)
    bf = lambda a: a.astype(BF16)
    l = 0

    widths = [d, d, d, GLA_DK, GLA_DK, GLA_DV, GLA_DV, GLA_RANK, MEM_WIDTH, MEM_WIDTH, N_BRANCH * d]
    offs = [int(o) for o in np.cumsum([0] + widths)]
    (o_ca, o_cb, o_cz, o_q, o_k, o_v, o_gz, o_al, o_mq, o_mz, o_g, o_end) = offs
    wt = w_in[l].T
    w2 = w_alpha2[l]
    w2_hi = bf(w2)
    w2_lo = bf(w2 - w2_hi.astype(F32))
    w2_cat = jnp.zeros((RANK_PAD, GLA_DK), BF16).at[:3 * GLA_RANK].set(
        jnp.concatenate([w2_hi, w2_hi, w2_lo], axis=0))

    w_qk, w_m = lax.optimization_barrier((wt[o_q:o_v], wt[o_al:o_g]))
    h, q, k, mq, mem_sz, alpha = _norm_qkm(xs, ln_in_g[l], bf(w_qk), bf(w_m), **TILES["norm_qkm"])

    u, conv_sz = _proj_call(_proj_conv_kernel, "proj_conv", h, wt, [o_ca, o_cb, o_cz], d,
                            [BF16, BF16], **TILES["proj_conv"])
    v, gla_sz = _proj_call(_proj_vz_kernel, "proj_vz", h, wt, [o_v, o_gz], GLA_DV,
                           [BF16, BF16], **TILES["proj_vz"])
    (gates,) = _proj_call(_proj_gate_kernel, "proj_gate", h, wt, [o_g], N_BRANCH * d, [BF16],
                          bias=b_gate[l], **TILES["proj_gate"])

    conv = _dwconv(u, dw_w[l], dw_b[l], batch, seq, **TILES["dwconv"])
    og = _gla(q, k, v, alpha, w2_cat, b_alpha[l], gla_sz, gla_norm_g[l], batch, seq)
    mkv = _mem_kv(mem2, mem_ln_g[l], bf(w_mem_kv[l]), batch, n_mem)

    y = _branches(conv, conv_sz, og, mq, mem_sz, mkv, gates, conv_ln_g[l], conv_ln_b[l],
                  bf(w_conv_out[l]), b_conv_out[l], bf(w_gla_out[l]), bf(w_mem_out[l]),
                  batch, seq, n_mem, **TILES["branches"])
    out = _out_proj(y, xs, bf(w_out[l]), final_g, **TILES["out_proj"])
    return out.reshape(batch, seq, d)
```

```python
import functools

import numpy as np
import jax
import jax.numpy as jnp
from jax import lax
from jax.experimental import pallas as pl
from jax.experimental.pallas import tpu as pltpu

F32 = jnp.float32
BF16 = jnp.bfloat16

D_MODEL = 2048
EPS = 1e-6
CONV_K = 31
GLA_HEADS = 4
GLA_DK = 1024
GLA_DV = 2048
GLA_DKH = GLA_DK // GLA_HEADS
GLA_DVH = GLA_DV // GLA_HEADS
GLA_RANK = 16
GLA_TAU = 16.0
MEM_HEADS = 4
MEM_HD = 128
MEM_WIDTH = MEM_HEADS * MEM_HD
N_BRANCH = 3

LANES = 128
VMEM_LIMIT = 60 * 1024 * 1024

GLA_CHUNK = 256
GLA_LEVELS = 8
GLA_CHUNKS_PER_STEP = 2
RANK_PAD = LANES
CONV_HALO = 32
CONV_TB = 16
LOG2_E = 1.4426950408889634
W_ROW_ALIGN = 16

TILES = dict(
    norm_qkm=dict(tm=512),
    proj_conv=dict(tm=1024, tn=512, weight_buffers=2),
    proj_vz=dict(tm=1024, tn=1024, weight_buffers=1),
    proj_gate=dict(tm=1024, tn=1024, weight_buffers=2),
    dwconv=dict(ts=512),
    branches=dict(tm=512),
    out_proj=dict(tm=512),
)


def _params(sem):
    return pltpu.CompilerParams(dimension_semantics=sem, vmem_limit_bytes=VMEM_LIMIT)


def _fixed(shape):
    index = (0,) * len(shape)
    return pl.BlockSpec(shape, lambda *_: index, pipeline_mode=pl.Buffered(1))


def _sigmoid(x):
    return 1.0 / (1.0 + jnp.exp(-x))


def _silu(x):
    return x * _sigmoid(x)


def _dot(a, b):
    return jnp.dot(a, b, preferred_element_type=F32)


def _dot_nt(a, b):
    return lax.dot_general(a, b, (((1,), (1,)), ((), ())), preferred_element_type=F32)


def _dot_tn(a, b):
    return lax.dot_general(a, b, (((0,), (0,)), ((), ())), preferred_element_type=F32)


def _rms_scale(x):
    return lax.rsqrt(jnp.mean(x * x, axis=-1, keepdims=True) + EPS)


def _round_weights_once(wt_refs, wb_ref):
    @pl.when(pl.program_id(1) == 0)
    def _():
        for n, wt_ref in enumerate(wt_refs):
            wb_ref[n] = wt_ref[...].T.astype(BF16)


def _proj_conv_kernel(h_ref, wa_ref, wb_ref, wz_ref, u_ref, sz_ref, w_ref):
    _round_weights_once((wa_ref, wb_ref, wz_ref), w_ref)
    h = h_ref[...]
    u_ref[...] = (_dot(h, w_ref[0]) * _sigmoid(_dot(h, w_ref[1]))).astype(BF16)
    sz_ref[...] = _silu(_dot(h, w_ref[2])).astype(BF16)


def _proj_vz_kernel(h_ref, wv_ref, wz_ref, v_ref, sz_ref, w_ref):
    _round_weights_once((wv_ref, wz_ref), w_ref)
    h = h_ref[...]
    v_ref[...] = _dot(h, w_ref[0]).astype(BF16)
    sz_ref[...] = _silu(_dot(h, w_ref[1])).astype(BF16)


def _proj_gate_kernel(h_ref, wg_ref, b_ref, g_ref, w_ref):
    _round_weights_once((wg_ref,), w_ref)
    g_ref[...] = _sigmoid(_dot(h_ref[...], w_ref[0]) + b_ref[...]).astype(BF16)


def _wt_rows(rows, d, index_map, buffers):
    def im(*idx):
        return pl.multiple_of(index_map(*idx), W_ROW_ALIGN), 0
    return pl.BlockSpec((pl.Element(rows), pl.Element(d)), im, pipeline_mode=pl.Buffered(buffers))


def _proj_call(kernel, name, h, wt, row_offsets, width, out_dtypes, tm, tn, bias=None,
               weight_buffers=2):
    t, d = h.shape
    assert all(off % W_ROW_ALIGN == 0 for off in row_offsets) and width % tn == 0
    in_specs = [pl.BlockSpec((tm, d), lambda j, i: (i, 0))]
    in_specs += [_wt_rows(tn, d, lambda j, i, o=off: o + j * tn, weight_buffers)
                 for off in row_offsets]
    args = [h] + [wt] * len(row_offsets)
    if bias is not None:
        in_specs.append(pl.BlockSpec((1, tn), lambda j, i: (0, j)))
        args.append(bias.reshape(1, width))
    return pl.pallas_call(
        kernel,
        grid=(width // tn, t // tm),
        in_specs=in_specs,
        out_specs=[pl.BlockSpec((tm, tn), lambda j, i: (i, j)) for _ in out_dtypes],
        out_shape=[jax.ShapeDtypeStruct((t, width), dt) for dt in out_dtypes],
        scratch_shapes=[pltpu.VMEM((len(row_offsets), d, tn), BF16)],
        compiler_params=_params(("arbitrary", "arbitrary")),
        name=name,
    )(*args)


def _norm_qkm_kernel(x_ref, g_ref, wqk_ref, wm_ref, h_ref, q_ref, k_ref, mq_ref, sz_ref, al_ref):
    x = x_ref[...]
    h = (x * _rms_scale(x) * g_ref[...]).astype(BF16)
    h_ref[...] = h
    al_ref[...] = _dot_nt(h, wm_ref[0:RANK_PAD])
    q_ref[...] = (_dot_nt(h, wqk_ref[0:GLA_DK]) * (GLA_DKH ** -0.5)).astype(BF16)
    k_ref[...] = _dot_nt(h, wqk_ref[GLA_DK:2 * GLA_DK]).astype(BF16)
    mq_ref[...] = _dot_nt(h, wm_ref[GLA_RANK:GLA_RANK + MEM_WIDTH]).astype(BF16)
    sz_ref[...] = _silu(_dot_nt(h, wm_ref[GLA_RANK + MEM_WIDTH:GLA_RANK + 2 * MEM_WIDTH])).astype(BF16)


def _norm_qkm(x, g, w_qk, w_m, tm):
    t, d = x.shape
    row = lambda i: (i, 0)
    out = lambda n: pl.BlockSpec((tm, n), row)
    return pl.pallas_call(
        _norm_qkm_kernel,
        grid=(t // tm,),
        in_specs=[pl.BlockSpec((tm, d), row), _fixed((1, d)), _fixed(w_qk.shape), _fixed(w_m.shape)],
        out_specs=[out(d), out(GLA_DK), out(GLA_DK), out(MEM_WIDTH), out(MEM_WIDTH), out(RANK_PAD)],
        out_shape=[jax.ShapeDtypeStruct((t, d), BF16),
                   jax.ShapeDtypeStruct((t, GLA_DK), BF16), jax.ShapeDtypeStruct((t, GLA_DK), BF16),
                   jax.ShapeDtypeStruct((t, MEM_WIDTH), BF16),
                   jax.ShapeDtypeStruct((t, MEM_WIDTH), BF16),
                   jax.ShapeDtypeStruct((t, RANK_PAD), F32)],
        compiler_params=_params(("parallel",)),
        name="norm_qkm",
    )(x, g.reshape(1, d), w_qk, w_m)


def _dwconv_kernel(u_ref, w_ref, b_ref, o_ref, ext_ref, acc_ref, *, ts):
    s = pl.program_id(1)
    sub = ext_ref.shape[1]

    @pl.when(s == 0)
    def _():
        ext_ref[0:CONV_HALO] = jnp.zeros((CONV_HALO,) + ext_ref.shape[1:], F32)

    @pl.when(s > 0)
    def _():
        ext_ref[0:CONV_HALO] = ext_ref[ts:ts + CONV_HALO]

    ext_ref[CONV_HALO:CONV_HALO + ts] = pltpu.einshape(
        "t(sl)->tsl", u_ref[...].astype(F32), s=sub)
    lead = CONV_HALO - (CONV_K - 1)

    def body(i, carry):
        t0 = pl.multiple_of(i * CONV_TB, CONV_TB)
        acc = jnp.broadcast_to(b_ref[...], (CONV_TB,) + ext_ref.shape[1:])
        for j in range(CONV_K):
            acc = acc + w_ref[j] * ext_ref[pl.ds(t0 + lead + j, CONV_TB)]
        acc_ref[pl.ds(t0, CONV_TB)] = acc
        return carry

    lax.fori_loop(0, ts // CONV_TB, body, 0)
    o_ref[...] = pltpu.einshape("tsl->t(sl)", acc_ref[...]).astype(o_ref.dtype)


def _dwconv(u, w, b, batch, seq, ts):
    t, c = u.shape
    sub = c // LANES
    n_s = seq // ts
    tok = lambda bi, si: (bi * n_s + si, 0)
    return pl.pallas_call(
        functools.partial(_dwconv_kernel, ts=ts),
        grid=(batch, n_s),
        in_specs=[pl.BlockSpec((ts, c), tok), _fixed((CONV_K, sub, LANES)), _fixed((1, sub, LANES))],
        out_specs=pl.BlockSpec((ts, c), tok),
        out_shape=jax.ShapeDtypeStruct((t, c), BF16),
        scratch_shapes=[pltpu.VMEM((ts + CONV_HALO, sub, LANES), F32),
                        pltpu.VMEM((ts, sub, LANES), F32)],
        compiler_params=_params(("arbitrary", "arbitrary")),
        name="dwconv",
    )(u, w.reshape(CONV_K, sub, LANES), b.reshape(1, sub, LANES))


def _gla_constants():
    c = GLA_CHUNK
    t = np.arange(c)[:, None]
    u = np.arange(c)[None, :]
    mats = np.zeros((GLA_LEVELS + 1, c, c), np.float32)
    mats[0] = u <= t
    for lvl in range(GLA_LEVELS):
        m = 1 << lvl
        pos = t % (2 * m)
        r = t - pos + m - 1
        mats[1 + lvl] = ((pos >= m) & (u > r) & (u <= t)) | ((pos < m) & (u > t) & (u <= r))
    x = t ^ u
    lv = np.where(x > 0, np.floor(np.log2(np.maximum(x, 1))).astype(np.int32), GLA_LEVELS)
    lv = np.where(u > t, -1, lv).astype(np.int32)
    return mats.reshape((GLA_LEVELS + 1) * c, c), lv


def _gla_kernel(q_ref, k_ref, v_ref, al_ref, w2_ref, ba_ref, gz_ref, ng_ref, mats_ref, lv_ref,
                o_ref, st_ref):
    c = GLA_CHUNK

    @pl.when(pl.program_id(1) == 0)
    def _():
        st_ref[...] = jnp.zeros_like(st_ref)

    al = al_ref[...]
    lane = lax.broadcasted_iota(jnp.int32, al.shape, 1)
    gate = jnp.where(lane < GLA_RANK, al, 0.0)
    hi = gate.astype(BF16).astype(F32)
    lo = gate - hi
    lhs = (hi + pltpu.roll(lo, GLA_RANK, 1) + pltpu.roll(hi, 2 * GLA_RANK, 1)).astype(BF16)
    x = _dot(lhs, w2_ref[...]) + ba_ref[...]
    log_a = (jnp.minimum(x, 0.0) - jnp.log1p(jnp.exp(-jnp.abs(x)))) * (1.0 / GLA_TAU)
    la = (log_a * LOG2_E).astype(BF16)

    lv = lv_ref[...]
    for hd in range(GLA_HEADS):
        ks = slice(hd * GLA_DKH, (hd + 1) * GLA_DKH)
        vs = slice(hd * GLA_DVH, (hd + 1) * GLA_DVH)
        st = st_ref[hd]
        for sc in range(q_ref.shape[0] // c):
            rows = slice(sc * c, (sc + 1) * c)
            q = q_ref[rows, ks]
            k = k_ref[rows, ks]
            v = v_ref[rows, vs]

            e_all = _dot(mats_ref[...], la[rows, ks])
            b2 = e_all[:c]
            b2_last = b2[c - 1:c, :]
            q_in = q * jnp.exp2(b2).astype(BF16)
            k_up = k * jnp.exp2(b2_last - b2).astype(BF16)

            o = _dot_nt(q_in, st.astype(BF16))

            p = jnp.where(lv == GLA_LEVELS, _dot_nt(q, k).astype(BF16), jnp.zeros((), BF16))
            for lvl in range(GLA_LEVELS):
                w = jnp.exp2(e_all[(1 + lvl) * c:(2 + lvl) * c]).astype(BF16)
                p = jnp.where(lv == lvl, _dot_nt(q * w, k * w).astype(BF16), p)
            o = o + _dot(p, v)

            st = st * jnp.exp2(b2_last) + _dot_tn(v, k_up)

            y = o * _rms_scale(o) * ng_ref[:, vs]
            o_ref[rows, vs] = (y * gz_ref[rows, vs].astype(F32)).astype(o_ref.dtype)
        st_ref[hd] = st


def _gla(q, k, v, alpha, w2_cat, b_alpha, gz, norm_g, batch, seq):
    t = q.shape[0]
    c = GLA_CHUNK * GLA_CHUNKS_PER_STEP
    n_c = seq // c
    mats, lv = _gla_constants()
    tok = lambda bi, ni: (bi * n_c + ni, 0)
    return pl.pallas_call(
        _gla_kernel,
        grid=(batch, n_c),
        in_specs=[pl.BlockSpec((c, GLA_DK), tok), pl.BlockSpec((c, GLA_DK), tok),
                  pl.BlockSpec((c, GLA_DV), tok), pl.BlockSpec((c, RANK_PAD), tok),
                  _fixed((RANK_PAD, GLA_DK)), _fixed((1, GLA_DK)),
                  pl.BlockSpec((c, GLA_DV), tok), _fixed((1, GLA_DV)),
                  _fixed(mats.shape), _fixed(lv.shape)],
        out_specs=pl.BlockSpec((c, GLA_DV), tok),
        out_shape=jax.ShapeDtypeStruct((t, GLA_DV), BF16),
        scratch_shapes=[pltpu.VMEM((GLA_HEADS, GLA_DVH, GLA_DKH), F32)],
        compiler_params=_params(("arbitrary", "arbitrary")),
        name="gla",
    )(q, k, v, alpha, w2_cat, b_alpha.reshape(1, GLA_DK), gz, norm_g.reshape(1, GLA_DV),
      jnp.asarray(mats, BF16), jnp.asarray(lv, BF16))


def _mem_kv_kernel(m_ref, g_ref, w_ref, o_ref):
    m = m_ref[...]
    mn = (m * _rms_scale(m) * g_ref[...]).astype(BF16)
    o_ref[...] = _dot(mn, w_ref[...]).astype(o_ref.dtype)


def _mem_kv(mem2, g, w, batch, n_mem):
    d = mem2.shape[1]
    n = w.shape[1]
    return pl.pallas_call(
        _mem_kv_kernel,
        grid=(batch,),
        in_specs=[pl.BlockSpec((n_mem, d), lambda bi: (bi, 0)), _fixed((1, d)), _fixed((d, n))],
        out_specs=pl.BlockSpec((n_mem, n), lambda bi: (bi, 0)),
        out_shape=jax.ShapeDtypeStruct((batch * n_mem, n), BF16),
        compiler_params=_params(("parallel",)),
        name="mem_kv",
    )(mem2, g.reshape(1, d), w)


def _branches_kernel(c_ref, csz_ref, og_ref, mq_ref, msz_ref, kv_ref, g_ref,
                     lg_ref, lb_ref, wc_ref, bc_ref, wg_ref, wm_ref, y_ref):
    d = D_MODEL
    c = c_ref[...].astype(F32)
    dev = c - jnp.mean(c, axis=-1, keepdims=True)
    var = jnp.mean(dev * dev, axis=-1, keepdims=True)
    yh = dev * lax.rsqrt(var + EPS) * lg_ref[...] + lb_ref[...]
    u = _silu(yh) * csz_ref[...].astype(F32)
    y = (_dot(u.astype(BF16), wc_ref[...]) + bc_ref[...]) * g_ref[:, 0:d].astype(F32)

    y = y + _dot(og_ref[...], wg_ref[...]) * g_ref[:, d:2 * d].astype(F32)

    q = mq_ref[...]
    kv = kv_ref[...]
    outs = []
    for hd in range(MEM_HEADS):
        lo, hi = hd * MEM_HD, (hd + 1) * MEM_HD
        sc = _dot_nt(q[:, lo:hi], kv[:, lo:hi]) * (MEM_HD ** -0.5)
        e = jnp.exp(sc - jnp.max(sc, axis=-1, keepdims=True))
        p = e / jnp.sum(e, axis=-1, keepdims=True)
        outs.append(_dot(p.astype(BF16), kv[:, MEM_WIDTH + lo:MEM_WIDTH + hi]))
    om = jnp.concatenate(outs, axis=-1) * msz_ref[...].astype(F32)
    y = y + _dot(om.astype(BF16), wm_ref[...]) * g_ref[:, 2 * d:3 * d].astype(F32)
    y_ref[...] = y.astype(y_ref.dtype)


def _branches(conv, conv_sz, og, mq, mem_sz, mkv, gates, ln_g, ln_b, w_conv, b_conv, w_gla,
              w_mem, batch, seq, n_mem, tm):
    t, d = conv.shape
    n_t = seq // tm
    tok = lambda bi, ti: (bi * n_t + ti, 0)
    wide = lambda w: pl.BlockSpec((tm, w), tok)
    return pl.pallas_call(
        _branches_kernel,
        grid=(batch, n_t),
        in_specs=[wide(d), wide(d), wide(GLA_DV), wide(MEM_WIDTH), wide(MEM_WIDTH),
                  pl.BlockSpec((n_mem, 2 * MEM_WIDTH), lambda bi, ti: (bi, 0)),
                  wide(N_BRANCH * d),
                  _fixed((1, d)), _fixed((1, d)), _fixed((d, d)), _fixed((1, d)),
                  _fixed((GLA_DV, d)), _fixed((MEM_WIDTH, d))],
        out_specs=wide(d),
        out_shape=jax.ShapeDtypeStruct((t, d), BF16),
        compiler_params=_params(("parallel", "parallel")),
        name="branches",
    )(conv, conv_sz, og, mq, mem_sz, mkv, gates, ln_g.reshape(1, d), ln_b.reshape(1, d),
      w_conv, b_conv.reshape(1, d), w_gla, w_mem)


def _out_kernel(y_ref, x_ref, w_ref, g_ref, o_ref):
    r = x_ref[...] + _dot(y_ref[...], w_ref[...])
    o_ref[...] = r * _rms_scale(r) * g_ref[...]


def _out_proj(y, x, w, g, tm):
    t, d = x.shape
    row = lambda i: (i, 0)
    return pl.pallas_call(
        _out_kernel,
        grid=(t // tm,),
        in_specs=[pl.BlockSpec((tm, d), row), pl.BlockSpec((tm, d), row), _fixed((d, d)),
                  _fixed((1, d))],
        out_specs=pl.BlockSpec((tm, d), row),
        out_shape=jax.ShapeDtypeStruct((t, d), F32),
        compiler_params=_params(("parallel",)),
        name="out_proj",
    )(y, x, w, g.reshape(1, d))


def kernel(x, mem, ln_in_g, mem_ln_g, w_in, b_gate, dw_w, dw_b, conv_ln_g, conv_ln_b,
           w_conv_out, b_conv_out, w_alpha2, b_alpha, gla_norm_g, w_gla_out,
           w_mem_kv, w_mem_out, w_out, final_g):
    batch, seq, d = x.shape
    n_mem = mem.shape[1]
    depth = w_in.shape[0]
    assert depth == 1 and d == D_MODEL
    t = batch * seq
    xs = x.reshape(t, d)
    mem2 = mem.reshape(batch * n_mem, d)
    bf = lambda a: a.astype(BF16)
    l = 0

    widths = [d, d, d, GLA_DK, GLA_DK, GLA_DV, GLA_DV, GLA_RANK, MEM_WIDTH, MEM_WIDTH, N_BRANCH * d]
    offs = [int(o) for o in np.cumsum([0] + widths)]
    (o_ca, o_cb, o_cz, o_q, o_k, o_v, o_gz, o_al, o_mq, o_mz, o_g, o_end) = offs
    wt = w_in[l].T
    w2 = w_alpha2[l]
    w2_hi = bf(w2)
    w2_lo = bf(w2 - w2_hi.astype(F32))
    w2_cat = jnp.zeros((RANK_PAD, GLA_DK), BF16).at[:3 * GLA_RANK].set(
        jnp.concatenate([w2_hi, w2_hi, w2_lo], axis=0))

    w_qk, w_m = lax.optimization_barrier((wt[o_q:o_v], wt[o_al:o_g]))
    h, q, k, mq, mem_sz, alpha = _norm_qkm(xs, ln_in_g[l], bf(w_qk), bf(w_m), **TILES["norm_qkm"])

    u, conv_sz = _proj_call(_proj_conv_kernel, "proj_conv", h, wt, [o_ca, o_cb, o_cz], d,
                            [BF16, BF16], **TILES["proj_conv"])
    v, gla_sz = _proj_call(_proj_vz_kernel, "proj_vz", h, wt, [o_v, o_gz], GLA_DV,
                           [BF16, BF16], **TILES["proj_vz"])
    (gates,) = _proj_call(_proj_gate_kernel, "proj_gate", h, wt, [o_g], N_BRANCH * d, [BF16],
                          bias=b_gate[l], **TILES["proj_gate"])

    conv = _dwconv(u, dw_w[l], dw_b[l], batch, seq, **TILES["dwconv"])
    og = _gla(q, k, v, alpha, w2_cat, b_alpha[l], gla_sz, gla_norm_g[l], batch, seq)
    mkv = _mem_kv(mem2, mem_ln_g[l], bf(w_mem_kv[l]), batch, n_mem)

    y = _branches(conv, conv_sz, og, mq, mem_sz, mkv, gates, conv_ln_g[l], conv_ln_b[l],
                  bf(w_conv_out[l]), b_conv_out[l], bf(w_gla_out[l]), bf(w_mem_out[l]),
                  batch, seq, n_mem, **TILES["branches"])
    out = _out_proj(y, xs, bf(w_out[l]), final_g, **TILES["out_proj"])
    return out.reshape(batch, seq, d)
```

```python
import functools

import numpy as np
import jax
import jax.numpy as jnp
from jax import lax
from jax.experimental import pallas as pl
from jax.experimental.pallas import tpu as pltpu

F32 = jnp.float32
BF16 = jnp.bfloat16

D_MODEL = 2048
EPS = 1e-6
CONV_K = 31
GLA_HEADS = 4
GLA_DK = 1024
GLA_DV = 2048
GLA_DKH = GLA_DK // GLA_HEADS
GLA_DVH = GLA_DV // GLA_HEADS
GLA_RANK = 16
GLA_TAU = 16.0
MEM_HEADS = 4
MEM_HD = 128
MEM_WIDTH = MEM_HEADS * MEM_HD
N_BRANCH = 3

LANES = 128
VMEM_LIMIT = 60 * 1024 * 1024

GLA_CHUNK = 256
GLA_LEVELS = 8
GLA_CHUNKS_PER_STEP = 2
RANK_PAD = LANES
CONV_HALO = 32
CONV_TB = 16
LOG2_E = 1.4426950408889634
W_ROW_ALIGN = 16

TILES = dict(
    norm_qkm=dict(tm=512),
    proj_conv=dict(tm=1024, tn=512, weight_buffers=2),
    proj_vz=dict(tm=1024, tn=1024, weight_buffers=1),
    proj_gate=dict(tm=1024, tn=1024, weight_buffers=2),
    dwconv=dict(ts=512),
    branches=dict(tm=512),
    out_proj=dict(tm=512),
)


def _params(sem):
    return pltpu.CompilerParams(dimension_semantics=sem, vmem_limit_bytes=VMEM_LIMIT)


def _fixed(shape):
    index = (0,) * len(shape)
    return pl.BlockSpec(shape, lambda *_: index, pipeline_mode=pl.Buffered(1))


def _sigmoid(x):
    return 1.0 / (1.0 + jnp.exp(-x))


def _silu(x):
    return x * _sigmoid(x)


def _dot(a, b):
    return jnp.dot(a, b, preferred_element_type=F32)


def _dot_nt(a, b):
    return lax.dot_general(a, b, (((1,), (1,)), ((), ())), preferred_element_type=F32)


def _dot_tn(a, b):
    return lax.dot_general(a, b, (((0,), (0,)), ((), ())), preferred_element_type=F32)


def _rms_scale(x):
    return lax.rsqrt(jnp.mean(x * x, axis=-1, keepdims=True) + EPS)


def _round_weights_once(wt_refs, wb_ref):
    @pl.when(pl.program_id(1) == 0)
    def _():
        for n, wt_ref in enumerate(wt_refs):
            wb_ref[n] = wt_ref[...].T.astype(BF16)


def _proj_conv_kernel(h_ref, wa_ref, wb_ref, wz_ref, u_ref, sz_ref, w_ref):
    _round_weights_once((wa_ref, wb_ref, wz_ref), w_ref)
    h = h_ref[...]
    u_ref[...] = (_dot(h, w_ref[0]) * _sigmoid(_dot(h, w_ref[1]))).astype(BF16)
    sz_ref[...] = _silu(_dot(h, w_ref[2])).astype(BF16)


def _proj_vz_kernel(h_ref, wv_ref, wz_ref, v_ref, sz_ref, w_ref):
    _round_weights_once((wv_ref, wz_ref), w_ref)
    h = h_ref[...]
    v_ref[...] = _dot(h, w_ref[0]).astype(BF16)
    sz_ref[...] = _silu(_dot(h, w_ref[1])).astype(BF16)


def _proj_gate_kernel(h_ref, wg_ref, b_ref, g_ref, w_ref):
    _round_weights_once((wg_ref,), w_ref)
    g_ref[...] = _sigmoid(_dot(h_ref[...], w_ref[0]) + b_ref[...]).astype(BF16)


def _wt_rows(rows, d, index_map, buffers):
    def im(*idx):
        return pl.multiple_of(index_map(*idx), W_ROW_ALIGN), 0
    return pl.BlockSpec((pl.Element(rows), pl.Element(d)), im, pipeline_mode=pl.Buffered(buffers))


def _proj_call(kernel, name, h, wt, row_offsets, width, out_dtypes, tm, tn, bias=None,
               weight_buffers=2):
    t, d = h.shape
    assert all(off % W_ROW_ALIGN == 0 for off in row_offsets) and width % tn == 0
    in_specs = [pl.BlockSpec((tm, d), lambda j, i: (i, 0))]
    in_specs += [_wt_rows(tn, d, lambda j, i, o=off: o + j * tn, weight_buffers)
                 for off in row_offsets]
    args = [h] + [wt] * len(row_offsets)
    if bias is not None:
        in_specs.append(pl.BlockSpec((1, tn), lambda j, i: (0, j)))
        args.append(bias.reshape(1, width))
    return pl.pallas_call(
        kernel,
        grid=(width // tn, t // tm),
        in_specs=in_specs,
        out_specs=[pl.BlockSpec((tm, tn), lambda j, i: (i, j)) for _ in out_dtypes],
        out_shape=[jax.ShapeDtypeStruct((t, width), dt) for dt in out_dtypes],
        scratch_shapes=[pltpu.VMEM((len(row_offsets), d, tn), BF16)],
        compiler_params=_params(("arbitrary", "arbitrary")),
        name=name,
    )(*args)


def _norm_qkm_kernel(x_ref, g_ref, wqk_ref, wm_ref, h_ref, q_ref, k_ref, mq_ref, sz_ref, al_ref):
    x = x_ref[...]
    h = (x * _rms_scale(x) * g_ref[...]).astype(BF16)
    h_ref[...] = h
    al_ref[...] = _dot_nt(h, wm_ref[0:RANK_PAD])
    q_ref[...] = (_dot_nt(h, wqk_ref[0:GLA_DK]) * (GLA_DKH ** -0.5)).astype(BF16)
    k_ref[...] = _dot_nt(h, wqk_ref[GLA_DK:2 * GLA_DK]).astype(BF16)
    mq_ref[...] = _dot_nt(h, wm_ref[GLA_RANK:GLA_RANK + MEM_WIDTH]).astype(BF16)
    sz_ref[...] = _silu(_dot_nt(h, wm_ref[GLA_RANK + MEM_WIDTH:GLA_RANK + 2 * MEM_WIDTH])).astype(BF16)


def _norm_qkm(x, g, w_qk, w_m, tm):
    t, d = x.shape
    row = lambda i: (i, 0)
    out = lambda n: pl.BlockSpec((tm, n), row)
    return pl.pallas_call(
        _norm_qkm_kernel,
        grid=(t // tm,),
        in_specs=[pl.BlockSpec((tm, d), row), _fixed((1, d)), _fixed(w_qk.shape), _fixed(w_m.shape)],
        out_specs=[out(d), out(GLA_DK), out(GLA_DK), out(MEM_WIDTH), out(MEM_WIDTH), out(RANK_PAD)],
        out_shape=[jax.ShapeDtypeStruct((t, d), BF16),
                   jax.ShapeDtypeStruct((t, GLA_DK), BF16), jax.ShapeDtypeStruct((t, GLA_DK), BF16),
                   jax.ShapeDtypeStruct((t, MEM_WIDTH), BF16),
                   jax.ShapeDtypeStruct((t, MEM_WIDTH), BF16),
                   jax.ShapeDtypeStruct((t, RANK_PAD), F32)],
        compiler_params=_params(("parallel",)),
        name="norm_qkm",
    )(x, g.reshape(1, d), w_qk, w_m)


def _dwconv_kernel(u_ref, w_ref, b_ref, o_ref, ext_ref, acc_ref, *, ts):
    s = pl.program_id(1)
    sub = ext_ref.shape[1]

    @pl.when(s == 0)
    def _():
        ext_ref[0:CONV_HALO] = jnp.zeros((CONV_HALO,) + ext_ref.shape[1:], F32)

    @pl.when(s > 0)
    def _():
        ext_ref[0:CONV_HALO] = ext_ref[ts:ts + CONV_HALO]

    ext_ref[CONV_HALO:CONV_HALO + ts] = pltpu.einshape(
        "t(sl)->tsl", u_ref[...].astype(F32), s=sub)
    lead = CONV_HALO - (CONV_K - 1)

    def body(i, carry):
        t0 = pl.multiple_of(i * CONV_TB, CONV_TB)
        acc = jnp.broadcast_to(b_ref[...], (CONV_TB,) + ext_ref.shape[1:])
        for j in range(CONV_K):
            acc = acc + w_ref[j] * ext_ref[pl.ds(t0 + lead + j, CONV_TB)]
        acc_ref[pl.ds(t0, CONV_TB)] = acc
        return carry

    lax.fori_loop(0, ts // CONV_TB, body, 0)
    o_ref[...] = pltpu.einshape("tsl->t(sl)", acc_ref[...]).astype(o_ref.dtype)


def _dwconv(u, w, b, batch, seq, ts):
    t, c = u.shape
    sub = c // LANES
    n_s = seq // ts
    tok = lambda bi, si: (bi * n_s + si, 0)
    return pl.pallas_call(
        functools.partial(_dwconv_kernel, ts=ts),
        grid=(batch, n_s),
        in_specs=[pl.BlockSpec((ts, c), tok), _fixed((CONV_K, sub, LANES)), _fixed((1, sub, LANES))],
        out_specs=pl.BlockSpec((ts, c), tok),
        out_shape=jax.ShapeDtypeStruct((t, c), BF16),
        scratch_shapes=[pltpu.VMEM((ts + CONV_HALO, sub, LANES), F32),
                        pltpu.VMEM((ts, sub, LANES), F32)],
        compiler_params=_params(("arbitrary", "arbitrary")),
        name="dwconv",
    )(u, w.reshape(CONV_K, sub, LANES), b.reshape(1, sub, LANES))


def _gla_constants():
    c = GLA_CHUNK
    t = np.arange(c)[:, None]
    u = np.arange(c)[None, :]
    mats = np.zeros((GLA_LEVELS + 1, c, c), np.float32)
    mats[0] = u <= t
    for lvl in range(GLA_LEVELS):
        m = 1 << lvl
        pos = t % (2 * m)
        r = t - pos + m - 1
        mats[1 + lvl] = ((pos >= m) & (u > r) & (u <= t)) | ((pos < m) & (u > t) & (u <= r))
    x = t ^ u
    lv = np.where(x > 0, np.floor(np.log2(np.maximum(x, 1))).astype(np.int32), GLA_LEVELS)
    lv = np.where(u > t, -1, lv).astype(np.int32)
    return mats.reshape((GLA_LEVELS + 1) * c, c), lv


def _gla_kernel(q_ref, k_ref, v_ref, al_ref, w2_ref, ba_ref, gz_ref, ng_ref, mats_ref, lv_ref,
                o_ref, st_ref):
    c = GLA_CHUNK

    @pl.when(pl.program_id(1) == 0)
    def _():
        st_ref[...] = jnp.zeros_like(st_ref)

    al = al_ref[...]
    lane = lax.broadcasted_iota(jnp.int32, al.shape, 1)
    gate = jnp.where(lane < GLA_RANK, al, 0.0)
    hi = gate.astype(BF16).astype(F32)
    lo = gate - hi
    lhs = (hi + pltpu.roll(lo, GLA_RANK, 1) + pltpu.roll(hi, 2 * GLA_RANK, 1)).astype(BF16)
    x = _dot(lhs, w2_ref[...]) + ba_ref[...]
    log_a = (jnp.minimum(x, 0.0) - jnp.log1p(jnp.exp(-jnp.abs(x)))) * (1.0 / GLA_TAU)
    la = (log_a * LOG2_E).astype(BF16)

    lv = lv_ref[...]
    for hd in range(GLA_HEADS):
        ks = slice(hd * GLA_DKH, (hd + 1) * GLA_DKH)
        vs = slice(hd * GLA_DVH, (hd + 1) * GLA_DVH)
        st = st_ref[hd]
        for sc in range(q_ref.shape[0] // c):
            rows = slice(sc * c, (sc + 1) * c)
            q = q_ref[rows, ks]
            k = k_ref[rows, ks]
            v = v_ref[rows, vs]

            e_all = _dot(mats_ref[...], la[rows, ks])
            b2 = e_all[:c]
            b2_last = b2[c - 1:c, :]
            q_in = q * jnp.exp2(b2).astype(BF16)
            k_up = k * jnp.exp2(b2_last - b2).astype(BF16)

            o = _dot_nt(q_in, st.astype(BF16))

            p = jnp.where(lv == GLA_LEVELS, _dot_nt(q, k).astype(BF16), jnp.zeros((), BF16))
            for lvl in range(GLA_LEVELS - 1):
                w = jnp.exp2(e_all[(1 + lvl) * c:(2 + lvl) * c]).astype(BF16)
                p = jnp.where(lv == lvl, _dot_nt(q * w, k * w).astype(BF16), p)
            hc = c // 2
            w = jnp.exp2(e_all[GLA_LEVELS * c:(GLA_LEVELS + 1) * c]).astype(BF16)
            blk = _dot_nt(q[hc:] * w[hc:], k[:hc] * w[:hc]).astype(BF16)
            p = jnp.concatenate([p[:hc], jnp.concatenate([blk, p[hc:, hc:]], axis=1)], axis=0)
            o = o + _dot(p, v)

            st = st * jnp.exp2(b2_last) + _dot_tn(v, k_up)

            y = o * _rms_scale(o) * ng_ref[:, vs]
            o_ref[rows, vs] = (y * gz_ref[rows, vs].astype(F32)).astype(o_ref.dtype)
        st_ref[hd] = st


def _gla(q, k, v, alpha, w2_cat, b_alpha, gz, norm_g, batch, seq):
    t = q.shape[0]
    c = GLA_CHUNK * GLA_CHUNKS_PER_STEP
    n_c = seq // c
    mats, lv = _gla_constants()
    tok = lambda bi, ni: (bi * n_c + ni, 0)
    return pl.pallas_call(
        _gla_kernel,
        grid=(batch, n_c),
        in_specs=[pl.BlockSpec((c, GLA_DK), tok), pl.BlockSpec((c, GLA_DK), tok),
                  pl.BlockSpec((c, GLA_DV), tok), pl.BlockSpec((c, RANK_PAD), tok),
                  _fixed((RANK_PAD, GLA_DK)), _fixed((1, GLA_DK)),
                  pl.BlockSpec((c, GLA_DV), tok), _fixed((1, GLA_DV)),
                  _fixed(mats.shape), _fixed(lv.shape)],
        out_specs=pl.BlockSpec((c, GLA_DV), tok),
        out_shape=jax.ShapeDtypeStruct((t, GLA_DV), BF16),
        scratch_shapes=[pltpu.VMEM((GLA_HEADS, GLA_DVH, GLA_DKH), F32)],
        compiler_params=_params(("arbitrary", "arbitrary")),
        name="gla",
    )(q, k, v, alpha, w2_cat, b_alpha.reshape(1, GLA_DK), gz, norm_g.reshape(1, GLA_DV),
      jnp.asarray(mats, BF16), jnp.asarray(lv, BF16))


def _mem_kv_kernel(m_ref, g_ref, w_ref, o_ref):
    m = m_ref[...]
    mn = (m * _rms_scale(m) * g_ref[...]).astype(BF16)
    o_ref[...] = _dot(mn, w_ref[...]).astype(o_ref.dtype)


def _mem_kv(mem2, g, w, batch, n_mem):
    d = mem2.shape[1]
    n = w.shape[1]
    return pl.pallas_call(
        _mem_kv_kernel,
        grid=(batch,),
        in_specs=[pl.BlockSpec((n_mem, d), lambda bi: (bi, 0)), _fixed((1, d)), _fixed((d, n))],
        out_specs=pl.BlockSpec((n_mem, n), lambda bi: (bi, 0)),
        out_shape=jax.ShapeDtypeStruct((batch * n_mem, n), BF16),
        compiler_params=_params(("parallel",)),
        name="mem_kv",
    )(mem2, g.reshape(1, d), w)


def _branches_kernel(c_ref, csz_ref, og_ref, mq_ref, msz_ref, kv_ref, g_ref,
                     lg_ref, lb_ref, wc_ref, bc_ref, wg_ref, wm_ref, y_ref):
    d = D_MODEL
    c = c_ref[...].astype(F32)
    dev = c - jnp.mean(c, axis=-1, keepdims=True)
    var = jnp.mean(dev * dev, axis=-1, keepdims=True)
    yh = dev * lax.rsqrt(var + EPS) * lg_ref[...] + lb_ref[...]
    u = _silu(yh) * csz_ref[...].astype(F32)
    y = (_dot(u.astype(BF16), wc_ref[...]) + bc_ref[...]) * g_ref[:, 0:d].astype(F32)

    y = y + _dot(og_ref[...], wg_ref[...]) * g_ref[:, d:2 * d].astype(F32)

    q = mq_ref[...]
    kv = kv_ref[...]
    outs = []
    for hd in range(MEM_HEADS):
        lo, hi = hd * MEM_HD, (hd + 1) * MEM_HD
        sc = _dot_nt(q[:, lo:hi], kv[:, lo:hi]) * (MEM_HD ** -0.5)
        e = jnp.exp(sc - jnp.max(sc, axis=-1, keepdims=True))
        p = e / jnp.sum(e, axis=-1, keepdims=True)
        outs.append(_dot(p.astype(BF16), kv[:, MEM_WIDTH + lo:MEM_WIDTH + hi]))
    om = jnp.concatenate(outs, axis=-1) * msz_ref[...].astype(F32)
    y = y + _dot(om.astype(BF16), wm_ref[...]) * g_ref[:, 2 * d:3 * d].astype(F32)
    y_ref[...] = y.astype(y_ref.dtype)


def _branches(conv, conv_sz, og, mq, mem_sz, mkv, gates, ln_g, ln_b, w_conv, b_conv, w_gla,
              w_mem, batch, seq, n_mem, tm):
    t, d = conv.shape
    n_t = seq // tm
    tok = lambda bi, ti: (bi * n_t + ti, 0)
    wide = lambda w: pl.BlockSpec((tm, w), tok)
    return pl.pallas_call(
        _branches_kernel,
        grid=(batch, n_t),
        in_specs=[wide(d), wide(d), wide(GLA_DV), wide(MEM_WIDTH), wide(MEM_WIDTH),
                  pl.BlockSpec((n_mem, 2 * MEM_WIDTH), lambda bi, ti: (bi, 0)),
                  wide(N_BRANCH * d),
                  _fixed((1, d)), _fixed((1, d)), _fixed((d, d)), _fixed((1, d)),
                  _fixed((GLA_DV, d)), _fixed((MEM_WIDTH, d))],
        out_specs=wide(d),
        out_shape=jax.ShapeDtypeStruct((t, d), BF16),
        compiler_params=_params(("parallel", "parallel")),
        name="branches",
    )(conv, conv_sz, og, mq, mem_sz, mkv, gates, ln_g.reshape(1, d), ln_b.reshape(1, d),
      w_conv, b_conv.reshape(1, d), w_gla, w_mem)


def _out_kernel(y_ref, x_ref, w_ref, g_ref, o_ref):
    r = x_ref[...] + _dot(y_ref[...], w_ref[...])
    o_ref[...] = r * _rms_scale(r) * g_ref[...]


def _out_proj(y, x, w, g, tm):
    t, d = x.shape
    row = lambda i: (i, 0)
    return pl.pallas_call(
        _out_kernel,
        grid=(t // tm,),
        in_specs=[pl.BlockSpec((tm, d), row), pl.BlockSpec((tm, d), row), _fixed((d, d)),
                  _fixed((1, d))],
        out_specs=pl.BlockSpec((tm, d), row),
        out_shape=jax.ShapeDtypeStruct((t, d), F32),
        compiler_params=_params(("parallel",)),
        name="out_proj",
    )(y, x, w, g.reshape(1, d))


def kernel(x, mem, ln_in_g, mem_ln_g, w_in, b_gate, dw_w, dw_b, conv_ln_g, conv_ln_b,
           w_conv_out, b_conv_out, w_alpha2, b_alpha, gla_norm_g, w_gla_out,
           w_mem_kv, w_mem_out, w_out, final_g):
    batch, seq, d = x.shape
    n_mem = mem.shape[1]
    depth = w_in.shape[0]
    assert depth == 1 and d == D_MODEL
    t = batch * seq
    xs = x.reshape(t, d)
    mem2 = mem.reshape(batch * n_mem, d)
    bf = lambda a: a.astype(BF16)
    l = 0

    widths = [d, d, d, GLA_DK, GLA_DK, GLA_DV, GLA_DV, GLA_RANK, MEM_WIDTH, MEM_WIDTH, N_BRANCH * d]
    offs = [int(o) for o in np.cumsum([0] + widths)]
    (o_ca, o_cb, o_cz, o_q, o_k, o_v, o_gz, o_al, o_mq, o_mz, o_g, o_end) = offs
    wt = w_in[l].T
    w2 = w_alpha2[l]
    w2_hi = bf(w2)
    w2_lo = bf(w2 - w2_hi.astype(F32))
    w2_cat = jnp.zeros((RANK_PAD, GLA_DK), BF16).at[:3 * GLA_RANK].set(
        jnp.concatenate([w2_hi, w2_hi, w2_lo], axis=0))

    w_qk, w_m = lax.optimization_barrier((wt[o_q:o_v], wt[o_al:o_g]))
    h, q, k, mq, mem_sz, alpha = _norm_qkm(xs, ln_in_g[l], bf(w_qk), bf(w_m), **TILES["norm_qkm"])

    u, conv_sz = _proj_call(_proj_conv_kernel, "proj_conv", h, wt, [o_ca, o_cb, o_cz], d,
                            [BF16, BF16], **TILES["proj_conv"])
    v, gla_sz = _proj_call(_proj_vz_kernel, "proj_vz", h, wt, [o_v, o_gz], GLA_DV,
                           [BF16, BF16], **TILES["proj_vz"])
    (gates,) = _proj_call(_proj_gate_kernel, "proj_gate", h, wt, [o_g], N_BRANCH * d, [BF16],
                          bias=b_gate[l], **TILES["proj_gate"])

    conv = _dwconv(u, dw_w[l], dw_b[l], batch, seq, **TILES["dwconv"])
    og = _gla(q, k, v, alpha, w2_cat, b_alpha[l], gla_sz, gla_norm_g[l], batch, seq)
    mkv = _mem_kv(mem2, mem_ln_g[l], bf(w_mem_kv[l]), batch, n_mem)

    y = _branches(conv, conv_sz, og, mq, mem_sz, mkv, gates, conv_ln_g[l], conv_ln_b[l],
                  bf(w_conv_out[l]), b_conv_out[l], bf(w_gla_out[l]), bf(w_mem_out[l]),
                  batch, seq, n_mem, **TILES["branches"])
    out = _out_proj(y, xs, bf(w_out[l]), final_g, **TILES["out_proj"])
    return out.reshape(batch, seq, d)
```
